```python
import math
import jax, jax.numpy as jnp
from jax import lax
import numpy as np

D_MODEL = 1024
BATCH = 4
SEQ = 4096
DEPTH = 2
DEC_BATCH = 2
DEC_SEQ = 16384
PAST_LEN = 128

HEAD_DIM = 64
MIX_WIDTH = D_MODEL
NA_HEADS = (MIX_WIDTH // 2) // HEAD_DIM
NA_WIDTH = NA_HEADS * HEAD_DIM
GRID_W = 64
NA_WIN_ROWS = 8
NA_WIN_COLS = 16
POOL_WIDTH = MIX_WIDTH - NA_WIDTH
POOL_WINDOWS = (2, 4, 8, 16)
POOL_GROUPS = len(POOL_WINDOWS)
POOL_GROUP_DIM = POOL_WIDTH // POOL_GROUPS
EVEN_IN = 3 * NA_WIDTH + POOL_WIDTH
SWA_HEADS = 8
SWA_KV_HEADS = 2
SWA_WIDTH = SWA_HEADS * HEAD_DIM
SWA_KV_WIDTH = SWA_KV_HEADS * HEAD_DIM
SWA_WINDOW = 128
SWA_BLOCK = 128
CONV_WIDTH = MIX_WIDTH - SWA_WIDTH
CONV_K = 3
ODD_IN = SWA_WIDTH + 2 * SWA_KV_WIDTH + 3 * CONV_WIDTH
D_FF = -(-(8 * D_MODEL) // (3 * 256)) * 256
ROPE_THETA = 10000.0
EPS = 1e-6
N_EVEN = (DEPTH + 1) // 2
N_ODD = DEPTH // 2

kernel_name = "hybrid_natten_pool_swa_conv_encoder"


def _rmsnorm(x, g):
    xf = x.astype(jnp.float32)
    y = xf * lax.rsqrt(jnp.mean(xf * xf, axis=-1, keepdims=True) + EPS)
    return (y * g.astype(jnp.float32)).astype(x.dtype)


def _modulate(h, shift, scale):
    return h * (1 + scale[:, None, :]) + shift[:, None, :]


def _rope(x):
    L = x.shape[1]
    inv = 1.0 / (ROPE_THETA ** (jnp.arange(0, HEAD_DIM, 2, dtype=jnp.float32) / HEAD_DIM))
    ang = jnp.arange(L, dtype=jnp.float32)[:, None] * inv[None, :]
    cos = jnp.cos(ang)[None, :, None, :]
    sin = jnp.sin(ang)[None, :, None, :]
    x1, x2 = jnp.split(x.astype(jnp.float32), 2, axis=-1)
    out = jnp.concatenate([x1 * cos - x2 * sin, x1 * sin + x2 * cos], axis=-1)
    return out.astype(x.dtype)


def _neighbourhood_attention(q, k, v, rpb):
    B, L, H, Dh = q.shape
    rows = L // GRID_W
    wr = min(NA_WIN_ROWS, rows)
    r = jnp.arange(rows)
    row_start = jnp.clip(r - wr // 2, 0, rows - wr)
    key_rows = row_start[:, None] + jnp.arange(wr)[None, :]
    cq = jnp.arange(GRID_W)
    col_start = jnp.clip(cq - NA_WIN_COLS // 2, 0, GRID_W - NA_WIN_COLS)
    kc = jnp.arange(GRID_W)
    col_ok = (kc[None, :] >= col_start[:, None]) & (kc[None, :] < col_start[:, None] + NA_WIN_COLS)
    qg = q.reshape(B, rows, GRID_W, H, Dh)
    kg = k.reshape(B, rows, GRID_W, H, Dh)[:, key_rows]
    vg = v.reshape(B, rows, GRID_W, H, Dh)[:, key_rows]
    s = jnp.einsum('brqhd,brikhd->bhrqik', qg, kg).astype(jnp.float32) * (Dh ** -0.5)
    dr = key_rows - r[:, None] + (NA_WIN_ROWS - 1)
    dc = jnp.clip(kc[None, :] - cq[:, None], -(NA_WIN_COLS - 1), NA_WIN_COLS - 1) + (NA_WIN_COLS - 1)
    bias = rpb.astype(jnp.float32)[:, dr][..., dc]
    bias = bias.transpose(0, 1, 3, 2, 4)
    s = jnp.where(col_ok[None, None, None, :, None, :], s + bias[None], -jnp.inf)
    p = jax.nn.softmax(s.reshape(B, H, rows, GRID_W, wr * GRID_W), axis=-1)
    p = p.reshape(B, H, rows, GRID_W, wr, GRID_W).astype(v.dtype)
    out = jnp.einsum('bhrqik,brikhd->brqhd', p, vg)
    return out.reshape(B, L, H * Dh)


def _multiscale_pool(u, w_grp, scale):
    B, L, _ = u.shape
    ug = u.reshape(B, L, POOL_GROUPS, POOL_GROUP_DIM)
    cs = jnp.cumsum(ug.astype(jnp.float32), axis=1)
    cs = jnp.pad(cs, ((0, 0), (1, 0), (0, 0), (0, 0)))
    t = jnp.arange(L)
    pooled = []
    for g, w in enumerate(POOL_WINDOWS):
        lo = jnp.clip(t - w // 2, 0, L)
        hi = jnp.clip(t - w // 2 + w, 0, L)
        csg = cs[:, :, g]
        cnt = (hi - lo).astype(jnp.float32)[None, :, None]
        pooled.append((csg[:, hi] - csg[:, lo]) / cnt)
    mixed = jnp.stack(pooled, axis=2) - ug.astype(jnp.float32)
    y = jnp.einsum('blgc,gce->blge', mixed.astype(u.dtype), w_grp)
    return y.reshape(B, L, POOL_WIDTH) * scale


def _window_gqa_sink(q, k, v, sink):
    B, L, Hq, Dh = q.shape
    Hkv = k.shape[2]
    G = Hq // Hkv
    nb = L // SWA_BLOCK
    qb = q.reshape(B, nb, SWA_BLOCK, Hkv, G, Dh)
    pad = ((0, 0), (SWA_BLOCK, SWA_BLOCK), (0, 0), (0, 0))
    kp = jnp.pad(k, pad)
    vp = jnp.pad(v, pad)
    idx = jnp.arange(nb)[:, None] * SWA_BLOCK + jnp.arange(3 * SWA_BLOCK)[None, :]
    kb = kp[:, idx]
    vb = vp[:, idx]
    s = jnp.einsum('bnqhgd,bnkhd->bhgnqk', qb, kb).astype(jnp.float32) * (Dh ** -0.5)
    qpos = jnp.arange(nb)[:, None] * SWA_BLOCK + jnp.arange(SWA_BLOCK)[None, :]
    kpos = idx - SWA_BLOCK
    ok = (jnp.abs(qpos[:, :, None] - kpos[:, None, :]) <= SWA_WINDOW) & (kpos >= 0)[:, None, :] & (kpos < L)[:, None, :]
    s = jnp.where(ok[None, None, None], s, -jnp.inf)
    sink_l = sink.astype(jnp.float32).reshape(Hkv, G)[None, :, :, None, None, None]
    m = jnp.maximum(jnp.max(s, axis=-1, keepdims=True), sink_l)
    e = jnp.exp(s - m)
    p = e / (jnp.sum(e, axis=-1, keepdims=True) + jnp.exp(sink_l - m))
    out = jnp.einsum('bhgnqk,bnkhd->bnqhgd', p.astype(v.dtype), vb)
    return out.reshape(B, L, Hq * Dh)


def _gated_short_conv(bg, cg, xin, w, b):
    u = cg * xin
    L = u.shape[1]
    up = jnp.pad(u, ((0, 0), (CONV_K // 2, CONV_K // 2), (0, 0)))
    conv = b
    for j in range(CONV_K):
        conv = conv + up[:, j:j + L] * w[j]
    return bg * conv


def _even_mixer(h, w_in, rpb, pool_w, pool_scale, w_out):
    B, L, _ = h.shape
    z = h @ w_in
    q, k, v, u = jnp.split(z, [NA_WIDTH, 2 * NA_WIDTH, 3 * NA_WIDTH], axis=-1)
    shp = (B, L, NA_HEADS, HEAD_DIM)
    a = _neighbourhood_attention(q.reshape(shp), k.reshape(shp), v.reshape(shp), rpb)
    p = _multiscale_pool(u, pool_w, pool_scale)
    return jnp.concatenate([a, p], axis=-1) @ w_out


def _odd_mixer(h, w_in, sink, conv_w, conv_b, w_out):
    B, L, _ = h.shape
    z = h @ w_in
    s1 = SWA_WIDTH
    s2 = s1 + SWA_KV_WIDTH
    s3 = s2 + SWA_KV_WIDTH
    s4 = s3 + CONV_WIDTH
    s5 = s4 + CONV_WIDTH
    q, k, v, bg, cg, xin = jnp.split(z, [s1, s2, s3, s4, s5], axis=-1)
    q = _rope(q.reshape(B, L, SWA_HEADS, HEAD_DIM))
    k = _rope(k.reshape(B, L, SWA_KV_HEADS, HEAD_DIM))
    v = v.reshape(B, L, SWA_KV_HEADS, HEAD_DIM)
    a = _window_gqa_sink(q, k, v, sink)
    d = _gated_short_conv(bg, cg, xin, conv_w, conv_b)
    return jnp.concatenate([a, d], axis=-1) @ w_out


def _swiglu(h, w1, w3, w2):
    return (jax.nn.silu(h @ w1) * (h @ w3)) @ w2


def _trunk(x, c, ada_w, ada_b, norm_g, final_g, ffn_w1, ffn_w3, ffn_w2,
           even_w_in, na_rpb, pool_w, pool_scale, even_w_out,
           odd_w_in, swa_sink, conv_w, conv_b, odd_w_out):
    c_act = jax.nn.silu(c)
    for i in range(DEPTH):
        mod = c_act @ ada_w[i] + ada_b[i]
        sh1, sc1, g1, sh2, sc2, g2 = jnp.split(mod, 6, axis=-1)
        h = _modulate(_rmsnorm(x, norm_g[i, 0]), sh1, sc1)
        j = i // 2
        if i % 2 == 0:
            mix = _even_mixer(h, even_w_in[j], na_rpb[j], pool_w[j], pool_scale[j], even_w_out[j])
        else:
            mix = _odd_mixer(h, odd_w_in[j], swa_sink[j], conv_w[j], conv_b[j], odd_w_out[j])
        x = x + g1[:, None, :] * mix
        h = _modulate(_rmsnorm(x, norm_g[i, 1]), sh2, sc2)
        x = x + g2[:, None, :] * _swiglu(h, ffn_w1[i], ffn_w3[i], ffn_w2[i])
    return _rmsnorm(x, final_g)


def setup_inputs(seed: int = 0) -> dict:
    key = jax.random.key(seed)
    ks = jax.random.split(key, 24)
    f32 = jnp.float32
    n = lambda k, s: jax.random.normal(k, s, dtype=f32)
    D = D_MODEL
    return {
        "x_prompt": n(ks[0], (BATCH, SEQ, D)),
        "x_sample": n(ks[1], (DEC_BATCH, DEC_SEQ, D)),
        "c_prompt": n(ks[2], (BATCH, D)),
        "c_sample": n(ks[3], (DEC_BATCH, D)),
        "ada_w": n(ks[4], (DEPTH, D, 6 * D)) * (0.5 * D ** -0.5),
        "ada_b": n(ks[5], (DEPTH, 6 * D)) * 0.01,
        "norm_g": 1.0 + 0.01 * n(ks[6], (DEPTH, 2, D)),
        "final_g": 1.0 + 0.01 * n(ks[7], (D,)),
        "ffn_w1": n(ks[8], (DEPTH, D, D_FF)) * D ** -0.5,
        "ffn_w3": n(ks[9], (DEPTH, D, D_FF)) * D ** -0.5,
        "ffn_w2": n(ks[10], (DEPTH, D_FF, D)) * D_FF ** -0.5,
        "even_w_in": n(ks[11], (N_EVEN, D, EVEN_IN)) * D ** -0.5,
        "na_rpb": n(ks[12], (N_EVEN, NA_HEADS, 2 * NA_WIN_ROWS - 1, 2 * NA_WIN_COLS - 1)) * 0.1,
        "pool_w": n(ks[13], (N_EVEN, POOL_GROUPS, POOL_GROUP_DIM, POOL_GROUP_DIM)) * POOL_GROUP_DIM ** -0.5,
        "pool_scale": 1.0 + 0.02 * n(ks[14], (N_EVEN, POOL_WIDTH)),
        "even_w_out": n(ks[15], (N_EVEN, MIX_WIDTH, D)) * MIX_WIDTH ** -0.5,
        "odd_w_in": n(ks[16], (N_ODD, D, ODD_IN)) * D ** -0.5,
        "swa_sink": n(ks[17], (N_ODD, SWA_HEADS)) * 0.5,
        "conv_w": n(ks[18], (N_ODD, CONV_K, CONV_WIDTH)) * CONV_K ** -0.5,
        "conv_b": n(ks[19], (N_ODD, CONV_WIDTH)) * 0.01,
        "odd_w_out": n(ks[20], (N_ODD, MIX_WIDTH, D)) * MIX_WIDTH ** -0.5,
    }


def reference(x_prompt, x_sample, c_prompt, c_sample, ada_w, ada_b, norm_g, final_g,
              ffn_w1, ffn_w3, ffn_w2, even_w_in, na_rpb, pool_w, pool_scale, even_w_out,
              odd_w_in, swa_sink, conv_w, conv_b, odd_w_out):
    y_prompt = _trunk(x_prompt, c_prompt, ada_w, ada_b, norm_g, final_g, ffn_w1, ffn_w3, ffn_w2,
                      even_w_in, na_rpb, pool_w, pool_scale, even_w_out,
                      odd_w_in, swa_sink, conv_w, conv_b, odd_w_out)
    y_sample = _trunk(x_sample, c_sample, ada_w, ada_b, norm_g, final_g, ffn_w1, ffn_w3, ffn_w2,
                      even_w_in, na_rpb, pool_w, pool_scale, even_w_out,
                      odd_w_in, swa_sink, conv_w, conv_b, odd_w_out)
    return (y_prompt, y_sample)
```

```python
import functools

import numpy as np
import jax
import jax.numpy as jnp
from jax import lax
from jax.experimental import pallas as pl
from jax.experimental.pallas import tpu as pltpu

F32 = jnp.float32
BF16 = jnp.bfloat16

D_MODEL = 1024
HEAD_DIM = 64
LANES = 128
GRID_W = 64
NA_WIN_ROWS = 8
NA_WIN_COLS = 16
NA_WIDTH = 512
NA_PAIRS = NA_WIDTH // LANES
POOL_WINDOWS = (2, 4, 8, 16)
POOL_GROUP_DIM = 128
POOL_HALO = 8
EVEN_IN = 2048
SWA_WIDTH = 512
SWA_KV_WIDTH = 128
SWA_WINDOW = 128
SWA_BLOCK = 128
CONV_WIDTH = 512
ODD_IN = 2304
ODD_OUT = 1792
D_FF = 2816
ROPE_THETA = 10000.0
EPS = 1e-6
NEG_INF = float("-inf")

ROW_TILE = 8
MIX_TILE = ROW_TILE * GRID_W
NA_KEY_ROWS = ROW_TILE + NA_WIN_ROWS - 1
PRE_TILE = 1024
POST_TILE = 512
VMEM_LIMIT = 56 * 1024 * 1024


def _cparams(n_axes):
    return pltpu.CompilerParams(
        dimension_semantics=("arbitrary",) * n_axes,
        vmem_limit_bytes=VMEM_LIMIT,
    )


def _const_spec(shape):
    nd = len(shape)
    return pl.BlockSpec(shape, lambda *_: (0,) * nd, pipeline_mode=pl.Buffered(1))


ADA_TILE = 1536


def _ada_kernel(c_ref, w_ref, b_ref, o_ref):
    c = c_ref[...]
    act = c * jax.nn.sigmoid(c)
    o_ref[0] = jnp.dot(act, w_ref[0], precision=lax.Precision.HIGHEST,
                       preferred_element_type=F32) + b_ref[0]


def _ada_mod(c_all, ada_w, ada_b):
    depth, d, n = ada_w.shape
    rows = c_all.shape[0]
    return pl.pallas_call(
        _ada_kernel,
        grid=(depth, n // ADA_TILE),
        in_specs=[
            pl.BlockSpec((rows, d), lambda l, j: (0, 0)),
            pl.BlockSpec((1, d, ADA_TILE), lambda l, j: (l, 0, j)),
            pl.BlockSpec((1, 1, ADA_TILE), lambda l, j: (l, 0, j)),
        ],
        out_specs=pl.BlockSpec((1, rows, ADA_TILE), lambda l, j: (l, 0, j)),
        out_shape=jax.ShapeDtypeStruct((depth, rows, n), F32),
        compiler_params=_cparams(2),
        name="ada_mod",
    )(c_all, ada_w, ada_b.reshape(depth, 1, n))


ROPE_TILE = 2048


def _rope_kernel(inv_ref, cos_ref, sin_ref):
    tile = cos_ref.shape[0]
    i = pl.program_id(0)
    pos = (lax.broadcasted_iota(jnp.int32, (tile, LANES), 0) + i * tile).astype(F32)
    ang = pos * inv_ref[...]
    lane = lax.broadcasted_iota(jnp.int32, (tile, LANES), 1)
    s = jnp.sin(ang)
    cos_ref[...] = jnp.cos(ang)
    sin_ref[...] = jnp.where(lane < LANES // 2, -s, s)


def _rope_tables(length):
    inv = 1.0 / (ROPE_THETA ** (jnp.arange(0, HEAD_DIM, 2, dtype=F32) / HEAD_DIM))
    inv_lanes = jnp.tile(inv, LANES // (HEAD_DIM // 2)).reshape(1, LANES)
    tile = min(ROPE_TILE, length)
    return pl.pallas_call(
        _rope_kernel,
        grid=(length // tile,),
        in_specs=[pl.BlockSpec((1, LANES), lambda i: (0, 0))],
        out_specs=[pl.BlockSpec((tile, LANES), lambda i: (i, 0))] * 2,
        out_shape=[jax.ShapeDtypeStruct((length, LANES), F32)] * 2,
        compiler_params=_cparams(1),
        name="rope_tables",
    )(inv_lanes)


RPB_ROWS = 2 * NA_WIN_ROWS - 1
RPB_COLS = 2 * NA_WIN_COLS - 1
NA_KEYS = NA_WIN_ROWS * GRID_W


def _bias_kernel(rep_ref, o_ref, w_scr):
    shape = (GRID_W, RPB_ROWS * GRID_W)
    qc = lax.broadcasted_iota(jnp.int32, shape, 0)
    kc = lax.broadcasted_iota(jnp.int32, shape, 1) % GRID_W
    dc = kc - qc + (NA_WIN_COLS - 1)
    cs = jnp.clip(qc - NA_WIN_COLS // 2, 0, GRID_W - NA_WIN_COLS)
    ok = (kc >= cs) & (kc < cs + NA_WIN_COLS)
    for hh in range(2):
        acc = jnp.zeros(shape, F32)
        for d in range(RPB_COLS):
            acc = jnp.where(dc == d, rep_ref[hh, d:d + 1, :], acc)
        w_scr[...] = jnp.where(ok, acc, NEG_INF)
        for d0 in range(NA_WIN_ROWS):
            o_ref[d0, 0, hh * GRID_W:(hh + 1) * GRID_W, :] = w_scr[:, d0 * GRID_W:d0 * GRID_W + NA_KEYS]


def _bias_slabs(rpb):
    heads = rpb.shape[0]
    rep = jnp.repeat(jnp.transpose(rpb, (0, 2, 1)), GRID_W, axis=-1)
    return pl.pallas_call(
        _bias_kernel,
        grid=(heads // 2,),
        in_specs=[pl.BlockSpec((2, RPB_COLS, RPB_ROWS * GRID_W), lambda p: (p, 0, 0))],
        out_specs=pl.BlockSpec((NA_WIN_ROWS, 1, 2 * GRID_W, NA_KEYS), lambda p: (0, p, 0, 0)),
        out_shape=jax.ShapeDtypeStruct((NA_WIN_ROWS, heads // 2, 2 * GRID_W, NA_KEYS), F32),
        scratch_shapes=[pltpu.VMEM((GRID_W, RPB_ROWS * GRID_W), F32)],
        compiler_params=_cparams(1),
        name="na_bias_slabs",
    )(rep)


def _norm_modulate(x, g, shift, scale):
    y = x * lax.rsqrt(jnp.mean(x * x, axis=-1, keepdims=True) + EPS) * g
    return y * (1.0 + scale) + shift


def _pre_even_kernel(x_ref, mod_ref, g_ref, w_ref, o_ref):
    h = _norm_modulate(x_ref[0], g_ref[...], mod_ref[0, 0:1, :], mod_ref[0, 1:2, :])
    z = jnp.dot(h.astype(BF16), w_ref[...], preferred_element_type=F32)
    o_ref[0] = z.astype(BF16)


def _pre_odd_kernel(x_ref, mod_ref, g_ref, w_ref, cos_ref, sin_ref, o_ref):
    h = _norm_modulate(x_ref[0], g_ref[...], mod_ref[0, 0:1, :], mod_ref[0, 1:2, :])
    z = jnp.dot(h.astype(BF16), w_ref[...], preferred_element_type=F32)
    c = cos_ref[...]
    s = sin_ref[...]

    def rope(t):
        return t * c + pltpu.roll(t, LANES // 2, 1) * s

    for blk in range(SWA_WIDTH // LANES):
        q = rope(z[:, blk * LANES:(blk + 1) * LANES]) * (HEAD_DIM ** -0.5)
        o_ref[0, :, blk * LANES:(blk + 1) * LANES] = q.astype(BF16)
    k0 = SWA_WIDTH
    v0 = k0 + SWA_KV_WIDTH
    b0 = v0 + SWA_KV_WIDTH
    c0 = b0 + CONV_WIDTH
    x0 = c0 + CONV_WIDTH
    o_ref[0, :, 512:1024] = z[:, b0:c0].astype(BF16)
    o_ref[0, :, 1024:1536] = (z[:, c0:x0] * z[:, x0:x0 + CONV_WIDTH]).astype(BF16)
    o_ref[0, :, 1536:1664] = rope(z[:, k0:v0]).astype(BF16)
    o_ref[0, :, 1664:1792] = z[:, v0:b0].astype(BF16)


def _pre(x, mod, g, w, rope=None):
    b, length, d = x.shape
    n_in = w.shape[1]
    tile = min(PRE_TILE, length)
    in_specs = [
        pl.BlockSpec((1, tile, d), lambda i, j: (i, j, 0)),
        pl.BlockSpec((1, 6, d), lambda i, j: (i, 0, 0)),
        _const_spec((1, d)),
        _const_spec((d, n_in)),
    ]
    args = [x, mod, g.reshape(1, d), w]
    if rope is None:
        kern, n_out = _pre_even_kernel, n_in
    else:
        kern, n_out = _pre_odd_kernel, ODD_OUT
        in_specs += [pl.BlockSpec((tile, LANES), lambda i, j: (j, 0))] * 2
        args += list(rope)
    return pl.pallas_call(
        kern,
        grid=(b, length // tile),
        in_specs=in_specs,
        out_specs=pl.BlockSpec((1, tile, n_out), lambda i, j: (i, j, 0)),
        out_shape=jax.ShapeDtypeStruct((b, length, n_out), BF16),
        compiler_params=_cparams(2),
        name="pre_even" if rope is None else "pre_odd",
    )(*args)


def _fill_ext(ext_ref, prev_ref, cur_ref, next_ref, j, n_tiles):
    t = cur_ref.shape[1]
    first = j == 0
    last = j == n_tiles - 1
    prev = prev_ref[0].astype(F32)
    nxt = next_ref[0].astype(F32)
    ext_ref[0:POOL_HALO, :] = jnp.where(first, 0.0, prev)
    ext_ref[POOL_HALO:POOL_HALO + t, :] = cur_ref[0].astype(F32)
    ext_ref[POOL_HALO + t:POOL_HALO + t + POOL_HALO, :] = jnp.where(last, 0.0, nxt)


def _mix_even_kernel(q_ref, k_ref, v_ref, bias_ref, up_ref, uc_ref, un_ref, pw_ref, ps_ref,
                     o_ref, ext_ref, *, rows, length):
    j = pl.program_id(1)
    n_tiles = pl.num_programs(1)
    row0 = j * ROW_TILE
    win0 = jnp.clip(row0 - NA_WIN_ROWS // 2, 0, rows - NA_KEY_ROWS)
    lane = lax.broadcasted_iota(jnp.int32, (GRID_W, LANES), 1)
    low = lane < HEAD_DIM

    def row_body(r, carry):
        row = row0 + r
        rs = jnp.clip(row - NA_WIN_ROWS // 2, 0, rows - NA_WIN_ROWS)
        koff = pl.multiple_of((rs - win0) * GRID_W, GRID_W)
        d0 = rs - row + (NA_WIN_ROWS - 1)
        qoff = pl.multiple_of(r * GRID_W, GRID_W)
        for p in range(NA_PAIRS):
            cols = slice(p * LANES, (p + 1) * LANES)
            qp = q_ref[0, pl.ds(qoff, GRID_W), cols]
            zero = jnp.zeros_like(qp)
            qbd = jnp.concatenate([jnp.where(low, qp, zero), jnp.where(low, zero, qp)], axis=0)
            kp = k_ref[0, pl.ds(koff, NA_KEYS), cols]
            s = lax.dot_general(qbd, kp, (((1,), (1,)), ((), ())), preferred_element_type=F32)
            s = s * (HEAD_DIM ** -0.5) + bias_ref[d0, p]
            m = jnp.max(s, axis=-1, keepdims=True)
            e = jnp.exp(s - m)
            l = jnp.sum(e, axis=-1, keepdims=True)
            vp = v_ref[0, pl.ds(koff, NA_KEYS), cols]
            o = jnp.dot(e.astype(BF16), vp, preferred_element_type=F32) / l
            o_ref[0, pl.ds(qoff, GRID_W), cols] = jnp.where(low, o[:GRID_W], o[GRID_W:]).astype(BF16)
        return carry

    lax.fori_loop(0, ROW_TILE, row_body, 0)

    _fill_ext(ext_ref, up_ref, uc_ref, un_ref, j, n_tiles)
    t = lax.broadcasted_iota(jnp.int32, (MIX_TILE, 1), 0) + j * MIX_TILE
    for g, w in enumerate(POOL_WINDOWS):
        cols = slice(g * POOL_GROUP_DIM, (g + 1) * POOL_GROUP_DIM)
        acc = ext_ref[POOL_HALO - w // 2:POOL_HALO - w // 2 + MIX_TILE, cols]
        for kk in range(1, w):
            off = POOL_HALO - w // 2 + kk
            acc = acc + ext_ref[off:off + MIX_TILE, cols]
        lo = jnp.clip(t - w // 2, 0, length)
        hi = jnp.clip(t - w // 2 + w, 0, length)
        cnt = (hi - lo).astype(F32)
        mixed = acc / cnt - ext_ref[POOL_HALO:POOL_HALO + MIX_TILE, cols]
        y = jnp.dot(mixed.astype(BF16), pw_ref[g], preferred_element_type=F32) * ps_ref[:, cols]
        o_ref[0, :, NA_WIDTH + g * POOL_GROUP_DIM:NA_WIDTH + (g + 1) * POOL_GROUP_DIM] = y.astype(BF16)


def _mix_even(z, bias, pool_w, pool_scale):
    b, length, _ = z.shape
    rows = length // GRID_W
    assert length % MIX_TILE == 0 and rows >= NA_KEY_ROWS
    n_tiles = length // MIX_TILE
    halo_blocks = MIX_TILE // POOL_HALO
    n_halo = length // POOL_HALO
    u_blk = 3 * NA_WIDTH // 512

    def win_map(i, j):
        win0 = jnp.clip(j * ROW_TILE - NA_WIN_ROWS // 2, 0, rows - NA_KEY_ROWS)
        return pl.multiple_of(win0 * GRID_W, GRID_W)

    win = (pl.Element(1), pl.Element(NA_KEY_ROWS * GRID_W), pl.Element(NA_WIDTH))
    in_specs = [
        pl.BlockSpec((1, MIX_TILE, NA_WIDTH), lambda i, j: (i, j, 0)),
        pl.BlockSpec(win, lambda i, j: (i, win_map(i, j), NA_WIDTH)),
        pl.BlockSpec(win, lambda i, j: (i, win_map(i, j), 2 * NA_WIDTH)),
        _const_spec(bias.shape),
        pl.BlockSpec((1, POOL_HALO, 512), lambda i, j: (i, jnp.maximum(j * halo_blocks - 1, 0), u_blk)),
        pl.BlockSpec((1, MIX_TILE, 512), lambda i, j: (i, j, u_blk)),
        pl.BlockSpec((1, POOL_HALO, 512),
                     lambda i, j: (i, jnp.minimum((j + 1) * halo_blocks, n_halo - 1), u_blk)),
        _const_spec(pool_w.shape),
        _const_spec((1, 512)),
    ]
    return pl.pallas_call(
        functools.partial(_mix_even_kernel, rows=rows, length=length),
        grid=(b, n_tiles),
        in_specs=in_specs,
        out_specs=pl.BlockSpec((1, MIX_TILE, D_MODEL), lambda i, j: (i, j, 0)),
        out_shape=jax.ShapeDtypeStruct((b, length, D_MODEL), BF16),
        scratch_shapes=[pltpu.VMEM((MIX_TILE + 2 * POOL_HALO, 512), F32)],
        compiler_params=_cparams(2),
        name="mix_even",
    )(z, z, z, bias, z, z, z, pool_w, pool_scale.reshape(1, 512))


SWA_KEYS = 3 * SWA_BLOCK
SWA_WIN = MIX_TILE + 2 * SWA_BLOCK
SWA_GROUP = 4
Q_BLOCKS = MIX_TILE // SWA_BLOCK


def _mix_odd_kernel(q_ref, kv_ref, sink_ref, bg_ref, up_ref, uc_ref, un_ref, cw_ref, cb_ref,
                    o_ref, ext_ref, *, length):
    j = pl.program_id(1)
    n_tiles = pl.num_programs(1)
    t0 = j * MIX_TILE
    win0 = jnp.clip(t0 - SWA_BLOCK, 0, length - SWA_WIN)
    lane = lax.broadcasted_iota(jnp.int32, (SWA_BLOCK, LANES), 1)
    qk_a = (lane // (HEAD_DIM // 2)) % 2 == 0
    v_a = lane < HEAD_DIM
    stack = SWA_GROUP * SWA_BLOCK
    qrow = lax.broadcasted_iota(jnp.int32, (stack, SWA_KEYS), 0) % SWA_BLOCK
    kcol = lax.broadcasted_iota(jnp.int32, (stack, SWA_KEYS), 1)

    def q_body(qb, carry):
        qs = t0 + qb * SWA_BLOCK
        ks = jnp.clip(qs - SWA_BLOCK, 0, length - SWA_KEYS)
        koff = pl.multiple_of(ks - win0, SWA_BLOCK)
        qoff = pl.multiple_of(qb * SWA_BLOCK, SWA_BLOCK)
        valid = jnp.abs((qs + qrow) - (ks + kcol)) <= SWA_WINDOW
        kp = kv_ref[0, pl.ds(koff, SWA_KEYS), 0:LANES]
        vp = kv_ref[0, pl.ds(koff, SWA_KEYS), LANES:2 * LANES]
        outs = []
        for kvh in range(2):
            parts = []
            for c in range(SWA_GROUP):
                qc = q_ref[0, pl.ds(qoff, SWA_BLOCK), c * LANES:(c + 1) * LANES]
                keep = qk_a if kvh == 0 else jnp.logical_not(qk_a)
                parts.append(jnp.where(keep, qc, jnp.zeros_like(qc)))
            qst = jnp.concatenate(parts, axis=0)
            s = lax.dot_general(qst, kp, (((1,), (1,)), ((), ())), preferred_element_type=F32)
            s = jnp.where(valid, s, NEG_INF)
            sink = sink_ref[kvh]
            m = jnp.maximum(jnp.max(s, axis=-1, keepdims=True), sink)
            e = jnp.exp(s - m)
            den = jnp.sum(e, axis=-1, keepdims=True) + jnp.exp(sink - m)
            outs.append(jnp.dot(e.astype(BF16), vp, preferred_element_type=F32) / den)
        for c in range(SWA_GROUP):
            rs = slice(c * SWA_BLOCK, (c + 1) * SWA_BLOCK)
            o_ref[0, pl.ds(qoff, SWA_BLOCK), c * LANES:(c + 1) * LANES] = (
                jnp.where(v_a, outs[0][rs], outs[1][rs]).astype(BF16))
        return carry

    lax.fori_loop(0, Q_BLOCKS, q_body, 0)

    _fill_ext(ext_ref, up_ref, uc_ref, un_ref, j, n_tiles)
    conv = cb_ref[...]
    for jj in range(3):
        off = POOL_HALO - 1 + jj
        conv = conv + ext_ref[off:off + MIX_TILE, :] * cw_ref[jj:jj + 1, :]
    o_ref[0, :, SWA_WIDTH:SWA_WIDTH + CONV_WIDTH] = (bg_ref[0].astype(F32) * conv).astype(BF16)


def _mix_odd(z, sink_rows, conv_w, conv_b):
    b, length, _ = z.shape
    assert length % MIX_TILE == 0 and length >= SWA_WIN
    n_tiles = length // MIX_TILE
    halo_blocks = MIX_TILE // POOL_HALO
    n_halo = length // POOL_HALO
    stack = SWA_GROUP * SWA_BLOCK

    def win_map(i, j):
        return pl.multiple_of(jnp.clip(j * MIX_TILE - SWA_BLOCK, 0, length - SWA_WIN), SWA_BLOCK)

    in_specs = [
        pl.BlockSpec((1, MIX_TILE, 512), lambda i, j: (i, j, 0)),
        pl.BlockSpec((pl.Element(1), pl.Element(SWA_WIN), pl.Element(2 * SWA_KV_WIDTH)),
                     lambda i, j: (i, win_map(i, j), ODD_OUT - 2 * SWA_KV_WIDTH)),
        _const_spec((2, stack, 1)),
        pl.BlockSpec((1, MIX_TILE, 512), lambda i, j: (i, j, 1)),
        pl.BlockSpec((1, POOL_HALO, 512), lambda i, j: (i, jnp.maximum(j * halo_blocks - 1, 0), 2)),
        pl.BlockSpec((1, MIX_TILE, 512), lambda i, j: (i, j, 2)),
        pl.BlockSpec((1, POOL_HALO, 512),
                     lambda i, j: (i, jnp.minimum((j + 1) * halo_blocks, n_halo - 1), 2)),
        _const_spec((3, CONV_WIDTH)),
        _const_spec((1, CONV_WIDTH)),
    ]
    return pl.pallas_call(
        functools.partial(_mix_odd_kernel, length=length),
        grid=(b, n_tiles),
        in_specs=in_specs,
        out_specs=pl.BlockSpec((1, MIX_TILE, D_MODEL), lambda i, j: (i, j, 0)),
        out_shape=jax.ShapeDtypeStruct((b, length, D_MODEL), BF16),
        scratch_shapes=[pltpu.VMEM((MIX_TILE + 2 * POOL_HALO, 512), F32)],
        compiler_params=_cparams(2),
        name="mix_odd",
    )(z, z, sink_rows, z, z, z, z, conv_w, conv_b.reshape(1, CONV_WIDTH))


def _post_kernel(x_ref, mix_ref, mod_ref, g_ref, wo_ref, w1_ref, w3_ref, w2_ref, fg_ref, o_ref,
                 *, final):
    mix = jnp.dot(mix_ref[0], wo_ref[...], preferred_element_type=F32)
    x1 = x_ref[0] + mod_ref[0, 2:3, :] * mix
    h = _norm_modulate(x1, g_ref[...], mod_ref[0, 3:4, :], mod_ref[0, 4:5, :]).astype(BF16)
    a = jnp.dot(h, w1_ref[...], preferred_element_type=F32)
    b = jnp.dot(h, w3_ref[...], preferred_element_type=F32)
    hid = (a * jax.nn.sigmoid(a) * b).astype(BF16)
    f = jnp.dot(hid, w2_ref[...], preferred_element_type=F32)
    x2 = x1 + mod_ref[0, 5:6, :] * f
    if final:
        x2 = x2 * lax.rsqrt(jnp.mean(x2 * x2, axis=-1, keepdims=True) + EPS) * fg_ref[...]
    o_ref[0] = x2


def _post(x, mix, mod, g, wo, w1, w3, w2, final_g, final):
    b, length, d = x.shape
    tile = min(POST_TILE, length)
    in_specs = [
        pl.BlockSpec((1, tile, d), lambda i, j: (i, j, 0)),
        pl.BlockSpec((1, tile, d), lambda i, j: (i, j, 0)),
        pl.BlockSpec((1, 6, d), lambda i, j: (i, 0, 0)),
        _const_spec((1, d)),
        _const_spec(wo.shape),
        _const_spec(w1.shape),
        _const_spec(w3.shape),
        _const_spec(w2.shape),
        _const_spec((1, d)),
    ]
    return pl.pallas_call(
        functools.partial(_post_kernel, final=final),
        grid=(b, length // tile),
        in_specs=in_specs,
        out_specs=pl.BlockSpec((1, tile, d), lambda i, j: (i, j, 0)),
        out_shape=jax.ShapeDtypeStruct((b, length, d), F32),
        compiler_params=_cparams(2),
        name="post",
    )(x, mix, mod, g.reshape(1, d), wo, w1, w3, w2, final_g.reshape(1, d))


def _odd_in_perm():
    half = HEAD_DIM // 2
    q_cols = []
    for c in range(SWA_GROUP):
        for seg in range(4):
            head = c if seg % 2 == 0 else SWA_GROUP + c
            base = head * HEAD_DIM + (half if seg >= 2 else 0)
            q_cols += list(range(base, base + half))
    k_cols = []
    for seg in range(4):
        base = SWA_WIDTH + (seg % 2) * HEAD_DIM + (half if seg >= 2 else 0)
        k_cols += list(range(base, base + half))
    rest = list(range(SWA_WIDTH + SWA_KV_WIDTH, ODD_IN))
    return np.asarray(q_cols + k_cols + rest, dtype=np.int32)


def _odd_out_perm():
    rows = []
    for c in range(SWA_GROUP):
        for head in (c, SWA_GROUP + c):
            rows += list(range(head * HEAD_DIM, (head + 1) * HEAD_DIM))
    rows += list(range(SWA_WIDTH, D_MODEL))
    return np.asarray(rows, dtype=np.int32)


def _sink_rows(sink):
    return jnp.repeat(sink.astype(F32).reshape(2, SWA_GROUP), SWA_BLOCK, axis=1)[..., None]


def _trunk(x, mods, p):
    depth = p["ffn_w1"].shape[0]
    for i in range(depth):
        jdx = i // 2
        mod = mods[i]
        if i % 2 == 0:
            z = _pre(x, mod, p["norm_g"][i, 0], p["even_w_in"][jdx])
            mix = _mix_even(z, p["na_bias"][jdx], p["pool_w"][jdx], p["pool_scale"][jdx])
            wo = p["even_w_out"][jdx]
        else:
            z = _pre(x, mod, p["norm_g"][i, 0], p["odd_w_in"][jdx], rope=p["rope"])
            mix = _mix_odd(z, p["sink_rows"][jdx], p["conv_w"][jdx], p["conv_b"][jdx])
            wo = p["odd_w_out"][jdx]
        x = _post(x, mix, mod, p["norm_g"][i, 1], wo, p["ffn_w1"][i], p["ffn_w3"][i], p["ffn_w2"][i],
                  p["final_g"], final=(i == depth - 1))
    return x


def kernel(x_prompt, x_sample, c_prompt, c_sample, ada_w, ada_b, norm_g, final_g, ffn_w1, ffn_w3, ffn_w2,
           even_w_in, na_rpb, pool_w, pool_scale, even_w_out, odd_w_in, swa_sink, conv_w, conv_b, odd_w_out):
    depth = ada_w.shape[0]
    nb_p, nb_s = c_prompt.shape[0], c_sample.shape[0]
    pad = (-(nb_p + nb_s)) % 8
    c_all = jnp.concatenate([c_prompt, c_sample, jnp.zeros((pad, D_MODEL), F32)], axis=0)
    mod = _ada_mod(c_all, ada_w, ada_b)
    mods_p = [mod[i, :nb_p].reshape(nb_p, 6, D_MODEL) for i in range(depth)]
    mods_s = [mod[i, nb_p:nb_p + nb_s].reshape(nb_s, 6, D_MODEL) for i in range(depth)]

    max_len = max(x_prompt.shape[1], x_sample.shape[1])
    p = {
        "norm_g": norm_g, "final_g": final_g,
        "ffn_w1": ffn_w1.astype(BF16), "ffn_w3": ffn_w3.astype(BF16), "ffn_w2": ffn_w2.astype(BF16),
        "even_w_in": even_w_in.astype(BF16), "even_w_out": even_w_out.astype(BF16),
        "na_bias": [_bias_slabs(na_rpb[j]) for j in range(na_rpb.shape[0])],
        "pool_w": pool_w.astype(BF16), "pool_scale": pool_scale,
        "odd_w_in": odd_w_in[:, :, _odd_in_perm()].astype(BF16),
        "odd_w_out": odd_w_out[:, _odd_out_perm(), :].astype(BF16),
        "sink_rows": [_sink_rows(swa_sink[j]) for j in range(swa_sink.shape[0])],
        "conv_w": conv_w, "conv_b": conv_b,
        "rope": _rope_tables(max_len),
    }
    return (_trunk(x_prompt, mods_p, p), _trunk(x_sample, mods_s, p))
```

```python
import functools

import numpy as np
import jax
import jax.numpy as jnp
from jax import lax
from jax.experimental import pallas as pl
from jax.experimental.pallas import tpu as pltpu

F32 = jnp.float32
BF16 = jnp.bfloat16

D_MODEL = 1024
HEAD_DIM = 64
LANES = 128
GRID_W = 64
NA_WIN_ROWS = 8
NA_WIN_COLS = 16
NA_WIDTH = 512
NA_PAIRS = NA_WIDTH // LANES
POOL_WINDOWS = (2, 4, 8, 16)
POOL_GROUP_DIM = 128
POOL_BLK = 128
POOL_PAD = 64
POOL_EDGE = 16
CONV_HALO = 8
EVEN_IN = 2048
SWA_WIDTH = 512
SWA_KV_WIDTH = 128
SWA_WINDOW = 128
SWA_BLOCK = 128
CONV_WIDTH = 512
ODD_IN = 2304
ODD_OUT = 1792
D_FF = 2816
ROPE_THETA = 10000.0
EPS = 1e-6
NEG_INF = float("-inf")

ROW_TILE = 8
MIX_TILE = ROW_TILE * GRID_W
NA_KEY_ROWS = ROW_TILE + NA_WIN_ROWS - 1
NA_LOOKAHEAD = 3
SWA_LOOKAHEAD = 2
POOL_LOOKAHEAD = 2
PRE_TILE = 1024
POST_TILE = 512
VMEM_LIMIT = 56 * 1024 * 1024


def _cparams(n_axes):
    return pltpu.CompilerParams(
        dimension_semantics=("arbitrary",) * n_axes,
        vmem_limit_bytes=VMEM_LIMIT,
    )


def _const_spec(shape):
    nd = len(shape)
    return pl.BlockSpec(shape, lambda *_: (0,) * nd, pipeline_mode=pl.Buffered(1))


ADA_TILE = 1536


def _ada_kernel(c_ref, w_ref, b_ref, o_ref):
    c = c_ref[...]
    act = c * jax.nn.sigmoid(c)
    o_ref[0] = jnp.dot(act, w_ref[0], precision=lax.Precision.HIGHEST,
                       preferred_element_type=F32) + b_ref[0]


def _ada_mod(c_all, ada_w, ada_b):
    depth, d, n = ada_w.shape
    rows = c_all.shape[0]
    return pl.pallas_call(
        _ada_kernel,
        grid=(depth, n // ADA_TILE),
        in_specs=[
            pl.BlockSpec((rows, d), lambda l, j: (0, 0)),
            pl.BlockSpec((1, d, ADA_TILE), lambda l, j: (l, 0, j)),
            pl.BlockSpec((1, 1, ADA_TILE), lambda l, j: (l, 0, j)),
        ],
        out_specs=pl.BlockSpec((1, rows, ADA_TILE), lambda l, j: (l, 0, j)),
        out_shape=jax.ShapeDtypeStruct((depth, rows, n), F32),
        compiler_params=_cparams(2),
        name="ada_mod",
    )(c_all, ada_w, ada_b.reshape(depth, 1, n))


ROPE_TILE = 2048


def _rope_kernel(inv_ref, cos_ref, sin_ref):
    tile = cos_ref.shape[0]
    i = pl.program_id(0)
    pos = (lax.broadcasted_iota(jnp.int32, (tile, LANES), 0) + i * tile).astype(F32)
    ang = pos * inv_ref[...]
    lane = lax.broadcasted_iota(jnp.int32, (tile, LANES), 1)
    s = jnp.sin(ang)
    cos_ref[...] = jnp.cos(ang)
    sin_ref[...] = jnp.where(lane < LANES // 2, -s, s)


def _rope_tables(length):
    inv = 1.0 / (ROPE_THETA ** (jnp.arange(0, HEAD_DIM, 2, dtype=F32) / HEAD_DIM))
    inv_lanes = jnp.tile(inv, LANES // (HEAD_DIM // 2)).reshape(1, LANES)
    tile = min(ROPE_TILE, length)
    return pl.pallas_call(
        _rope_kernel,
        grid=(length // tile,),
        in_specs=[pl.BlockSpec((1, LANES), lambda i: (0, 0))],
        out_specs=[pl.BlockSpec((tile, LANES), lambda i: (i, 0))] * 2,
        out_shape=[jax.ShapeDtypeStruct((length, LANES), F32)] * 2,
        compiler_params=_cparams(1),
        name="rope_tables",
    )(inv_lanes)


RPB_ROWS = 2 * NA_WIN_ROWS - 1
RPB_COLS = 2 * NA_WIN_COLS - 1
NA_KEYS = NA_WIN_ROWS * GRID_W


def _bias_kernel(rep_ref, o_ref, w_scr):
    shape = (GRID_W, RPB_ROWS * GRID_W)
    qc = lax.broadcasted_iota(jnp.int32, shape, 0)
    kc = lax.broadcasted_iota(jnp.int32, shape, 1) % GRID_W
    dc = kc - qc + (NA_WIN_COLS - 1)
    cs = jnp.clip(qc - NA_WIN_COLS // 2, 0, GRID_W - NA_WIN_COLS)
    ok = (kc >= cs) & (kc < cs + NA_WIN_COLS)
    for hh in range(2):
        acc = jnp.zeros(shape, F32)
        for d in range(RPB_COLS):
            acc = jnp.where(dc == d, rep_ref[hh, d:d + 1, :], acc)
        w_scr[...] = jnp.where(ok, acc, NEG_INF)
        for d0 in range(NA_WIN_ROWS):
            o_ref[d0, 0, hh * GRID_W:(hh + 1) * GRID_W, :] = w_scr[:, d0 * GRID_W:d0 * GRID_W + NA_KEYS]


def _bias_slabs(rpb):
    heads = rpb.shape[0]
    rep = jnp.repeat(jnp.transpose(rpb, (0, 2, 1)), GRID_W, axis=-1)
    return pl.pallas_call(
        _bias_kernel,
        grid=(heads // 2,),
        in_specs=[pl.BlockSpec((2, RPB_COLS, RPB_ROWS * GRID_W), lambda p: (p, 0, 0))],
        out_specs=pl.BlockSpec((NA_WIN_ROWS, 1, 2 * GRID_W, NA_KEYS), lambda p: (0, p, 0, 0)),
        out_shape=jax.ShapeDtypeStruct((NA_WIN_ROWS, heads // 2, 2 * GRID_W, NA_KEYS), F32),
        scratch_shapes=[pltpu.VMEM((GRID_W, RPB_ROWS * GRID_W), F32)],
        compiler_params=_cparams(1),
        name="na_bias_slabs",
    )(rep)


def _norm_modulate(x, g, shift, scale):
    y = x * lax.rsqrt(jnp.mean(x * x, axis=-1, keepdims=True) + EPS) * g
    return y * (1.0 + scale) + shift


def _pre_even_kernel(x_ref, mod_ref, g_ref, w_ref, o_ref):
    h = _norm_modulate(x_ref[0], g_ref[...], mod_ref[0, 0:1, :], mod_ref[0, 1:2, :])
    z = jnp.dot(h.astype(BF16), w_ref[...], preferred_element_type=F32)
    o_ref[0] = z.astype(BF16)


def _pre_odd_kernel(x_ref, mod_ref, g_ref, w_ref, cos_ref, sin_ref, o_ref):
    h = _norm_modulate(x_ref[0], g_ref[...], mod_ref[0, 0:1, :], mod_ref[0, 1:2, :])
    z = jnp.dot(h.astype(BF16), w_ref[...], preferred_element_type=F32)
    c = cos_ref[...]
    s = sin_ref[...]

    def rope(t):
        return t * c + pltpu.roll(t, LANES // 2, 1) * s

    for blk in range(SWA_WIDTH // LANES):
        q = rope(z[:, blk * LANES:(blk + 1) * LANES])
        o_ref[0, :, blk * LANES:(blk + 1) * LANES] = q.astype(BF16)
    k0 = SWA_WIDTH
    v0 = k0 + SWA_KV_WIDTH
    b0 = v0 + SWA_KV_WIDTH
    c0 = b0 + CONV_WIDTH
    x0 = c0 + CONV_WIDTH
    o_ref[0, :, 512:1024] = z[:, b0:c0].astype(BF16)
    o_ref[0, :, 1024:1536] = (z[:, c0:x0] * z[:, x0:x0 + CONV_WIDTH]).astype(BF16)
    o_ref[0, :, 1536:1664] = rope(z[:, k0:v0]).astype(BF16)
    o_ref[0, :, 1664:1792] = z[:, v0:b0].astype(BF16)


def _pre(x, mod, g, w, rope=None):
    b, length, d = x.shape
    n_in = w.shape[1]
    tile = min(PRE_TILE, length)
    in_specs = [
        pl.BlockSpec((1, tile, d), lambda i, j: (i, j, 0)),
        pl.BlockSpec((1, 6, d), lambda i, j: (i, 0, 0)),
        _const_spec((1, d)),
        _const_spec((d, n_in)),
    ]
    args = [x, mod, g.reshape(1, d), w]
    if rope is None:
        kern, n_out = _pre_even_kernel, n_in
    else:
        kern, n_out = _pre_odd_kernel, ODD_OUT
        in_specs += [pl.BlockSpec((tile, LANES), lambda i, j: (j, 0))] * 2
        args += list(rope)
    return pl.pallas_call(
        kern,
        grid=(b, length // tile),
        in_specs=in_specs,
        out_specs=pl.BlockSpec((1, tile, n_out), lambda i, j: (i, j, 0)),
        out_shape=jax.ShapeDtypeStruct((b, length, n_out), BF16),
        compiler_params=_cparams(2),
        name="pre_even" if rope is None else "pre_odd",
    )(*args)


def _fill_ext(ext_ref, prev_ref, cur_ref, next_ref, j, n_tiles):
    t = cur_ref.shape[1]
    first = j == 0
    last = j == n_tiles - 1
    prev = prev_ref[0].astype(F32)
    nxt = next_ref[0].astype(F32)
    ext_ref[0:CONV_HALO, :] = jnp.where(first, 0.0, prev)
    ext_ref[CONV_HALO:CONV_HALO + t, :] = cur_ref[0].astype(F32)
    ext_ref[CONV_HALO + t:CONV_HALO + t + CONV_HALO, :] = jnp.where(last, 0.0, nxt)


def _interleave(main, side):
    out, done = [], 0
    for i, task in enumerate(main):
        out.append(task)
        want = (i + 1) * len(side) // len(main)
        out.extend(side[done:want])
        done = want
    return out


def _mix_even_kernel(q_ref, k_ref, v_ref, bias_ref, up_ref, uc_ref, un_ref, pw_ref, ps_ref,
                     o_ref, ext_ref, *, rows, length):
    j = pl.program_id(1)
    n_tiles = pl.num_programs(1)
    row0 = j * ROW_TILE
    win0 = jnp.clip(row0 - NA_WIN_ROWS // 2, 0, rows - NA_KEY_ROWS)
    lane = lax.broadcasted_iota(jnp.int32, (GRID_W, LANES), 1)
    low = lane < HEAD_DIM
    ones_keys = jnp.ones((NA_KEYS, LANES), BF16)

    koffs, d0s = [], []
    for r in range(ROW_TILE):
        row = row0 + r
        rs = jnp.clip(row - NA_WIN_ROWS // 2, 0, rows - NA_WIN_ROWS)
        koffs.append(pl.multiple_of((rs - win0) * GRID_W, GRID_W))
        d0s.append(rs - row + (NA_WIN_ROWS - 1))

    def scores(r, p):
        cols = slice(p * LANES, (p + 1) * LANES)
        qp = q_ref[0, r * GRID_W:(r + 1) * GRID_W, cols]
        zero = jnp.zeros_like(qp)
        qbd = jnp.concatenate([jnp.where(low, qp, zero), jnp.where(low, zero, qp)], axis=0)
        kp = k_ref[0, pl.ds(koffs[r], NA_KEYS), cols]
        s = lax.dot_general(qbd, kp, (((1,), (1,)), ((), ())), preferred_element_type=F32)
        return s + bias_ref[d0s[r], p]

    def attend(r, p, s):
        cols = slice(p * LANES, (p + 1) * LANES)
        m = jnp.max(s, axis=-1, keepdims=True)
        e = jnp.exp(s - m).astype(BF16)
        vp = jnp.concatenate([v_ref[0, pl.ds(koffs[r], NA_KEYS), cols], ones_keys], axis=1)
        oa = jnp.dot(e, vp, preferred_element_type=F32)
        o = oa[:, :LANES] * (1.0 / oa[:, LANES:])
        o_ref[0, r * GRID_W:(r + 1) * GRID_W, cols] = jnp.where(low, o[:GRID_W], o[GRID_W:]).astype(BF16)

    first = j == 0
    last = j == n_tiles - 1
    zpad = jnp.zeros((POOL_PAD - POOL_EDGE, 512), BF16)
    prev = up_ref[0]
    nxt = un_ref[0]
    ext_ref[0:POOL_PAD - POOL_EDGE, :] = zpad
    ext_ref[POOL_PAD - POOL_EDGE:POOL_PAD, :] = jnp.where(first, jnp.zeros_like(prev), prev)
    ext_ref[POOL_PAD:POOL_PAD + MIX_TILE, :] = uc_ref[0]
    ext_ref[POOL_PAD + MIX_TILE:POOL_PAD + MIX_TILE + POOL_EDGE, :] = jnp.where(last, jnp.zeros_like(nxt), nxt)
    ext_ref[POOL_PAD + MIX_TILE + POOL_EDGE:, :] = zpad
    band_shape = (POOL_BLK, POOL_BLK + 2 * POOL_PAD)
    delta = (lax.broadcasted_iota(jnp.int32, band_shape, 1)
             - lax.broadcasted_iota(jnp.int32, band_shape, 0) - POOL_PAD)
    bands = [jnp.where((delta >= -(w // 2)) & (delta < w - w // 2), 1.0, 0.0).astype(BF16)
             for w in POOL_WINDOWS]

    def pool_sum(blk, g):
        cols = slice(g * POOL_GROUP_DIM, (g + 1) * POOL_GROUP_DIM)
        base = blk * POOL_BLK
        x = ext_ref[base:base + POOL_BLK + 2 * POOL_PAD, cols]
        return jnp.dot(bands[g], x, preferred_element_type=F32)

    def pool_project(blk, g, acc):
        w = POOL_WINDOWS[g]
        cols = slice(g * POOL_GROUP_DIM, (g + 1) * POOL_GROUP_DIM)
        base = blk * POOL_BLK
        u = ext_ref[base + POOL_PAD:base + POOL_PAD + POOL_BLK, cols].astype(F32)
        t = lax.broadcasted_iota(jnp.int32, (POOL_BLK, 1), 0) + (j * MIX_TILE + base)
        lo = jnp.clip(t - w // 2, 0, length)
        hi = jnp.clip(t - w // 2 + w, 0, length)
        mixed = acc / (hi - lo).astype(F32) - u
        y = jnp.dot(mixed.astype(BF16), pw_ref[g], preferred_element_type=F32) * ps_ref[:, cols]
        o_ref[0, base:base + POOL_BLK, NA_WIDTH + g * POOL_GROUP_DIM:NA_WIDTH + (g + 1) * POOL_GROUP_DIM] = (
            y.astype(BF16))

    units = [(r, p) for r in range(ROW_TILE) for p in range(NA_PAIRS)]
    pending = {}

    def step(idx):
        if idx < len(units):
            pending[idx] = scores(*units[idx])
        if idx >= NA_LOOKAHEAD:
            done = idx - NA_LOOKAHEAD
            attend(*units[done], pending.pop(done))

    pools = [(blk, g) for blk in range(MIX_TILE // POOL_BLK) for g in range(len(POOL_WINDOWS))]
    sums = {}

    def pool_step(idx):
        if idx < len(pools):
            sums[idx] = pool_sum(*pools[idx])
        if idx >= POOL_LOOKAHEAD:
            done = idx - POOL_LOOKAHEAD
            pool_project(*pools[done], sums.pop(done))

    main = [functools.partial(step, idx) for idx in range(len(units) + NA_LOOKAHEAD)]
    side = [functools.partial(pool_step, idx) for idx in range(len(pools) + POOL_LOOKAHEAD)]
    for task in _interleave(main, side):
        task()


def _mix_even(z, bias, pool_w, pool_scale):
    b, length, _ = z.shape
    rows = length // GRID_W
    assert length % MIX_TILE == 0 and rows >= NA_KEY_ROWS
    n_tiles = length // MIX_TILE
    halo_blocks = MIX_TILE // POOL_EDGE
    n_halo = length // POOL_EDGE
    u_blk = 3 * NA_WIDTH // 512

    def win_map(i, j):
        win0 = jnp.clip(j * ROW_TILE - NA_WIN_ROWS // 2, 0, rows - NA_KEY_ROWS)
        return pl.multiple_of(win0 * GRID_W, GRID_W)

    win = (pl.Element(1), pl.Element(NA_KEY_ROWS * GRID_W), pl.Element(NA_WIDTH))
    in_specs = [
        pl.BlockSpec((1, MIX_TILE, NA_WIDTH), lambda i, j: (i, j, 0)),
        pl.BlockSpec(win, lambda i, j: (i, win_map(i, j), NA_WIDTH)),
        pl.BlockSpec(win, lambda i, j: (i, win_map(i, j), 2 * NA_WIDTH)),
        _const_spec(bias.shape),
        pl.BlockSpec((1, POOL_EDGE, 512), lambda i, j: (i, jnp.maximum(j * halo_blocks - 1, 0), u_blk)),
        pl.BlockSpec((1, MIX_TILE, 512), lambda i, j: (i, j, u_blk)),
        pl.BlockSpec((1, POOL_EDGE, 512),
                     lambda i, j: (i, jnp.minimum((j + 1) * halo_blocks, n_halo - 1), u_blk)),
        _const_spec(pool_w.shape),
        _const_spec((1, 512)),
    ]
    return pl.pallas_call(
        functools.partial(_mix_even_kernel, rows=rows, length=length),
        grid=(b, n_tiles),
        in_specs=in_specs,
        out_specs=pl.BlockSpec((1, MIX_TILE, D_MODEL), lambda i, j: (i, j, 0)),
        out_shape=jax.ShapeDtypeStruct((b, length, D_MODEL), BF16),
        scratch_shapes=[pltpu.VMEM((MIX_TILE + 2 * POOL_PAD, 512), BF16)],
        compiler_params=_cparams(2),
        name="mix_even",
    )(z, z, z, bias, z, z, z, pool_w, pool_scale.reshape(1, 512))


SWA_KEYS = 3 * SWA_BLOCK
SWA_WIN = MIX_TILE + 2 * SWA_BLOCK
SWA_GROUP = 4
Q_BLOCKS = MIX_TILE // SWA_BLOCK


def _mix_odd_kernel(q_ref, kv_ref, sink_ref, mask_ref, bg_ref, up_ref, uc_ref, un_ref, cw_ref, cb_ref,
                    o_ref, ext_ref, *, length):
    j = pl.program_id(1)
    n_tiles = pl.num_programs(1)
    t0 = j * MIX_TILE
    win0 = jnp.clip(t0 - SWA_BLOCK, 0, length - SWA_WIN)
    lane = lax.broadcasted_iota(jnp.int32, (SWA_BLOCK, LANES), 1)
    qk_a = (lane // (HEAD_DIM // 2)) % 2 == 0
    v_a = lane < HEAD_DIM
    stack = SWA_GROUP * SWA_BLOCK
    ones_keys = jnp.ones((SWA_KEYS, LANES), BF16)
    koffs, rels = [], []
    for qb in range(Q_BLOCKS):
        qs = t0 + qb * SWA_BLOCK
        ks = jnp.clip(qs - SWA_BLOCK, 0, length - SWA_KEYS)
        koffs.append(pl.multiple_of(ks - win0, SWA_BLOCK))
        rels.append((qs - ks) // SWA_BLOCK)

    def scores(qb, c):
        rows_q = slice(qb * SWA_BLOCK, (qb + 1) * SWA_BLOCK)
        qc = q_ref[0, rows_q, c * LANES:(c + 1) * LANES]
        zero = jnp.zeros_like(qc)
        qst = jnp.concatenate([jnp.where(qk_a, qc, zero), jnp.where(qk_a, zero, qc)], axis=0)
        kp = kv_ref[0, pl.ds(koffs[qb], SWA_KEYS), 0:LANES]
        return lax.dot_general(qst, kp, (((1,), (1,)), ((), ())), preferred_element_type=F32)

    def attend(qb, c, s):
        rows_q = slice(qb * SWA_BLOCK, (qb + 1) * SWA_BLOCK)
        mask = mask_ref[rels[qb]]
        s = jnp.concatenate([s[:SWA_BLOCK] + mask, s[SWA_BLOCK:] + mask], axis=0)
        sink = jnp.concatenate([sink_ref[c * SWA_BLOCK:(c + 1) * SWA_BLOCK],
                                sink_ref[stack + c * SWA_BLOCK:stack + (c + 1) * SWA_BLOCK]], axis=0)
        m = jnp.maximum(jnp.max(s, axis=-1, keepdims=True), sink)
        e = jnp.exp(s - m).astype(BF16)
        vp = jnp.concatenate([kv_ref[0, pl.ds(koffs[qb], SWA_KEYS), LANES:2 * LANES], ones_keys], axis=1)
        oa = jnp.dot(e, vp, preferred_element_type=F32)
        o = oa[:, :LANES] * (1.0 / (oa[:, LANES:] + jnp.exp(sink - m)))
        o_ref[0, rows_q, c * LANES:(c + 1) * LANES] = (
            jnp.where(v_a, o[:SWA_BLOCK], o[SWA_BLOCK:]).astype(BF16))

    _fill_ext(ext_ref, up_ref, uc_ref, un_ref, j, n_tiles)

    def conv_chunk(blk):
        rows_c = slice(blk * SWA_BLOCK, (blk + 1) * SWA_BLOCK)
        conv = cb_ref[...]
        for jj in range(3):
            off = CONV_HALO - 1 + jj + blk * SWA_BLOCK
            conv = conv + ext_ref[off:off + SWA_BLOCK, :] * cw_ref[jj:jj + 1, :]
        o_ref[0, rows_c, SWA_WIDTH:SWA_WIDTH + CONV_WIDTH] = (
            bg_ref[0, rows_c, :].astype(F32) * conv).astype(BF16)

    units = [(qb, c) for qb in range(Q_BLOCKS) for c in range(SWA_GROUP)]
    pending = {}

    def step(idx):
        if idx < len(units):
            pending[idx] = scores(*units[idx])
        if idx >= SWA_LOOKAHEAD:
            done = idx - SWA_LOOKAHEAD
            attend(*units[done], pending.pop(done))

    main = [functools.partial(step, idx) for idx in range(len(units) + SWA_LOOKAHEAD)]
    side = [functools.partial(conv_chunk, blk) for blk in range(Q_BLOCKS)]
    for task in _interleave(main, side):
        task()


def _swa_masks():
    q = np.arange(SWA_BLOCK)[:, None]
    k = np.arange(SWA_KEYS)[None, :]
    slabs = [np.where(np.abs(q + rel * SWA_BLOCK - k) <= SWA_WINDOW, 0.0, -np.inf) for rel in range(3)]
    return jnp.asarray(np.stack(slabs), F32)


def _mix_odd(z, sink_rows, conv_w, conv_b):
    b, length, _ = z.shape
    assert length % MIX_TILE == 0 and length >= SWA_WIN
    n_tiles = length // MIX_TILE
    halo_blocks = MIX_TILE // CONV_HALO
    n_halo = length // CONV_HALO
    stack = SWA_GROUP * SWA_BLOCK

    def win_map(i, j):
        return pl.multiple_of(jnp.clip(j * MIX_TILE - SWA_BLOCK, 0, length - SWA_WIN), SWA_BLOCK)

    in_specs = [
        pl.BlockSpec((1, MIX_TILE, 512), lambda i, j: (i, j, 0)),
        pl.BlockSpec((pl.Element(1), pl.Element(SWA_WIN), pl.Element(2 * SWA_KV_WIDTH)),
                     lambda i, j: (i, win_map(i, j), ODD_OUT - 2 * SWA_KV_WIDTH)),
        _const_spec((2 * stack, 1)),
        _const_spec((3, SWA_BLOCK, SWA_KEYS)),
        pl.BlockSpec((1, MIX_TILE, 512), lambda i, j: (i, j, 1)),
        pl.BlockSpec((1, CONV_HALO, 512), lambda i, j: (i, jnp.maximum(j * halo_blocks - 1, 0), 2)),
        pl.BlockSpec((1, MIX_TILE, 512), lambda i, j: (i, j, 2)),
        pl.BlockSpec((1, CONV_HALO, 512),
                     lambda i, j: (i, jnp.minimum((j + 1) * halo_blocks, n_halo - 1), 2)),
        _const_spec((3, CONV_WIDTH)),
        _const_spec((1, CONV_WIDTH)),
    ]
    return pl.pallas_call(
        functools.partial(_mix_odd_kernel, length=length),
        grid=(b, n_tiles),
        in_specs=in_specs,
        out_specs=pl.BlockSpec((1, MIX_TILE, D_MODEL), lambda i, j: (i, j, 0)),
        out_shape=jax.ShapeDtypeStruct((b, length, D_MODEL), BF16),
        scratch_shapes=[pltpu.VMEM((MIX_TILE + 2 * CONV_HALO, 512), F32)],
        compiler_params=_cparams(2),
        name="mix_odd",
    )(z, z, sink_rows, _swa_masks(), z, z, z, z, conv_w, conv_b.reshape(1, CONV_WIDTH))


def _post_kernel(x_ref, mix_ref, mod_ref, g_ref, wo_ref, w1_ref, w3_ref, w2_ref, fg_ref, o_ref,
                 *, final):
    mix = jnp.dot(mix_ref[0], wo_ref[...], preferred_element_type=F32)
    x1 = x_ref[0] + mod_ref[0, 2:3, :] * mix
    h = _norm_modulate(x1, g_ref[...], mod_ref[0, 3:4, :], mod_ref[0, 4:5, :]).astype(BF16)
    a = jnp.dot(h, w1_ref[...], preferred_element_type=F32)
    b = jnp.dot(h, w3_ref[...], preferred_element_type=F32)
    hid = (a * jax.nn.sigmoid(a) * b).astype(BF16)
    f = jnp.dot(hid, w2_ref[...], preferred_element_type=F32)
    x2 = x1 + mod_ref[0, 5:6, :] * f
    if final:
        x2 = x2 * lax.rsqrt(jnp.mean(x2 * x2, axis=-1, keepdims=True) + EPS) * fg_ref[...]
    o_ref[0] = x2


def _post(x, mix, mod, g, wo, w1, w3, w2, final_g, final):
    b, length, d = x.shape
    tile = min(POST_TILE, length)
    in_specs = [
        pl.BlockSpec((1, tile, d), lambda i, j: (i, j, 0)),
        pl.BlockSpec((1, tile, d), lambda i, j: (i, j, 0)),
        pl.BlockSpec((1, 6, d), lambda i, j: (i, 0, 0)),
        _const_spec((1, d)),
        _const_spec(wo.shape),
        _const_spec(w1.shape),
        _const_spec(w3.shape),
        _const_spec(w2.shape),
        _const_spec((1, d)),
    ]
    return pl.pallas_call(
        functools.partial(_post_kernel, final=final),
        grid=(b, length // tile),
        in_specs=in_specs,
        out_specs=pl.BlockSpec((1, tile, d), lambda i, j: (i, j, 0)),
        out_shape=jax.ShapeDtypeStruct((b, length, d), F32),
        compiler_params=_cparams(2),
        name="post",
    )(x, mix, mod, g.reshape(1, d), wo, w1, w3, w2, final_g.reshape(1, d))


def _odd_in_perm():
    half = HEAD_DIM // 2
    q_cols = []
    for c in range(SWA_GROUP):
        for seg in range(4):
            head = c if seg % 2 == 0 else SWA_GROUP + c
            base = head * HEAD_DIM + (half if seg >= 2 else 0)
            q_cols += list(range(base, base + half))
    k_cols = []
    for seg in range(4):
        base = SWA_WIDTH + (seg % 2) * HEAD_DIM + (half if seg >= 2 else 0)
        k_cols += list(range(base, base + half))
    rest = list(range(SWA_WIDTH + SWA_KV_WIDTH, ODD_IN))
    return np.asarray(q_cols + k_cols + rest, dtype=np.int32)


def _odd_out_perm():
    rows = []
    for c in range(SWA_GROUP):
        for head in (c, SWA_GROUP + c):
            rows += list(range(head * HEAD_DIM, (head + 1) * HEAD_DIM))
    rows += list(range(SWA_WIDTH, D_MODEL))
    return np.asarray(rows, dtype=np.int32)


def _fold_qk_scale(w_in, q_width):
    scale = HEAD_DIM ** -0.5
    assert np.log2(scale) == round(np.log2(scale))
    col = jnp.arange(w_in.shape[-1]) < q_width
    return w_in * jnp.where(col, scale, 1.0).astype(w_in.dtype)


def _sink_rows(sink):
    return jnp.repeat(sink.astype(F32), SWA_BLOCK).reshape(2 * SWA_GROUP * SWA_BLOCK, 1)


def _trunk(x, mods, p):
    depth = p["ffn_w1"].shape[0]
    for i in range(depth):
        jdx = i // 2
        mod = mods[i]
        if i % 2 == 0:
            z = _pre(x, mod, p["norm_g"][i, 0], p["even_w_in"][jdx])
            mix = _mix_even(z, p["na_bias"][jdx], p["pool_w"][jdx], p["pool_scale"][jdx])
            wo = p["even_w_out"][jdx]
        else:
            z = _pre(x, mod, p["norm_g"][i, 0], p["odd_w_in"][jdx], rope=p["rope"])
            mix = _mix_odd(z, p["sink_rows"][jdx], p["conv_w"][jdx], p["conv_b"][jdx])
            wo = p["odd_w_out"][jdx]
        x = _post(x, mix, mod, p["norm_g"][i, 1], wo, p["ffn_w1"][i], p["ffn_w3"][i], p["ffn_w2"][i],
                  p["final_g"], final=(i == depth - 1))
    return x


def kernel(x_prompt, x_sample, c_prompt, c_sample, ada_w, ada_b, norm_g, final_g, ffn_w1, ffn_w3, ffn_w2,
           even_w_in, na_rpb, pool_w, pool_scale, even_w_out, odd_w_in, swa_sink, conv_w, conv_b, odd_w_out):
    depth = ada_w.shape[0]
    nb_p, nb_s = c_prompt.shape[0], c_sample.shape[0]
    pad = (-(nb_p + nb_s)) % 8
    c_all = jnp.concatenate([c_prompt, c_sample, jnp.zeros((pad, D_MODEL), F32)], axis=0)
    mod = _ada_mod(c_all, ada_w, ada_b)
    mods_p = [mod[i, :nb_p].reshape(nb_p, 6, D_MODEL) for i in range(depth)]
    mods_s = [mod[i, nb_p:nb_p + nb_s].reshape(nb_s, 6, D_MODEL) for i in range(depth)]

    max_len = max(x_prompt.shape[1], x_sample.shape[1])
    p = {
        "norm_g": norm_g, "final_g": final_g,
        "ffn_w1": ffn_w1.astype(BF16), "ffn_w3": ffn_w3.astype(BF16), "ffn_w2": ffn_w2.astype(BF16),
        "even_w_in": _fold_qk_scale(even_w_in, NA_WIDTH).astype(BF16),
        "even_w_out": even_w_out.astype(BF16),
        "na_bias": [_bias_slabs(na_rpb[j]) for j in range(na_rpb.shape[0])],
        "pool_w": pool_w.astype(BF16), "pool_scale": pool_scale,
        "odd_w_in": _fold_qk_scale(odd_w_in, SWA_WIDTH)[:, :, _odd_in_perm()].astype(BF16),
        "odd_w_out": odd_w_out[:, _odd_out_perm(), :].astype(BF16),
        "sink_rows": [_sink_rows(swa_sink[j]) for j in range(swa_sink.shape[0])],
        "conv_w": conv_w, "conv_b": conv_b,
        "rope": _rope_tables(max_len),
    }
    return (_trunk(x_prompt, mods_p, p), _trunk(x_sample, mods_s, p))
```

```python
import functools

import numpy as np
import jax
import jax.numpy as jnp
from jax import lax
from jax.experimental import pallas as pl
from jax.experimental.pallas import tpu as pltpu

F32 = jnp.float32
BF16 = jnp.bfloat16

D_MODEL = 1024
HEAD_DIM = 64
LANES = 128
GRID_W = 64
NA_WIN_ROWS = 8
NA_WIN_COLS = 16
NA_WIDTH = 512
NA_PAIRS = NA_WIDTH // LANES
POOL_WINDOWS = (2, 4, 8, 16)
POOL_GROUP_DIM = 128
POOL_BLK = 128
POOL_PAD = 64
POOL_EDGE = 16
CONV_HALO = 8
EVEN_IN = 2048
SWA_WIDTH = 512
SWA_KV_WIDTH = 128
SWA_WINDOW = 128
SWA_BLOCK = 128
CONV_WIDTH = 512
ODD_IN = 2304
ODD_OUT = 1792
D_FF = 2816
ROPE_THETA = 10000.0
EPS = 1e-6
NEG_INF = float("-inf")

ROW_TILE = 8
MIX_TILE = ROW_TILE * GRID_W
NA_KEY_ROWS = ROW_TILE + NA_WIN_ROWS - 1
NA_LOOKAHEAD = 3
SWA_LOOKAHEAD = 2
POOL_LOOKAHEAD = 2
PRE_TILE = 1024
VMEM_LIMIT = 56 * 1024 * 1024


def _cparams(n_axes):
    return pltpu.CompilerParams(
        dimension_semantics=("arbitrary",) * n_axes,
        vmem_limit_bytes=VMEM_LIMIT,
    )


def _const_spec(shape):
    nd = len(shape)
    return pl.BlockSpec(shape, lambda *_: (0,) * nd, pipeline_mode=pl.Buffered(1))


ADA_TILE = 1536


def _ada_kernel(c_ref, w_ref, b_ref, o_ref):
    c = c_ref[...]
    act = c * jax.nn.sigmoid(c)
    o_ref[0] = jnp.dot(act, w_ref[0], precision=lax.Precision.HIGHEST,
                       preferred_element_type=F32) + b_ref[0]


def _ada_mod(c_all, ada_w, ada_b):
    depth, d, n = ada_w.shape
    rows = c_all.shape[0]
    return pl.pallas_call(
        _ada_kernel,
        grid=(depth, n // ADA_TILE),
        in_specs=[
            pl.BlockSpec((rows, d), lambda l, j: (0, 0)),
            pl.BlockSpec((1, d, ADA_TILE), lambda l, j: (l, 0, j)),
            pl.BlockSpec((1, 1, ADA_TILE), lambda l, j: (l, 0, j)),
        ],
        out_specs=pl.BlockSpec((1, rows, ADA_TILE), lambda l, j: (l, 0, j)),
        out_shape=jax.ShapeDtypeStruct((depth, rows, n), F32),
        compiler_params=_cparams(2),
        name="ada_mod",
    )(c_all, ada_w, ada_b.reshape(depth, 1, n))


ROPE_TILE = 2048


def _rope_kernel(inv_ref, cos_ref, sin_ref):
    tile = cos_ref.shape[0]
    i = pl.program_id(0)
    pos = (lax.broadcasted_iota(jnp.int32, (tile, LANES), 0) + i * tile).astype(F32)
    ang = pos * inv_ref[...]
    lane = lax.broadcasted_iota(jnp.int32, (tile, LANES), 1)
    s = jnp.sin(ang)
    cos_ref[...] = jnp.cos(ang)
    sin_ref[...] = jnp.where(lane < LANES // 2, -s, s)


def _rope_tables(length):
    inv = 1.0 / (ROPE_THETA ** (jnp.arange(0, HEAD_DIM, 2, dtype=F32) / HEAD_DIM))
    inv_lanes = jnp.tile(inv, LANES // (HEAD_DIM // 2)).reshape(1, LANES)
    tile = min(ROPE_TILE, length)
    return pl.pallas_call(
        _rope_kernel,
        grid=(length // tile,),
        in_specs=[pl.BlockSpec((1, LANES), lambda i: (0, 0))],
        out_specs=[pl.BlockSpec((tile, LANES), lambda i: (i, 0))] * 2,
        out_shape=[jax.ShapeDtypeStruct((length, LANES), F32)] * 2,
        compiler_params=_cparams(1),
        name="rope_tables",
    )(inv_lanes)


RPB_ROWS = 2 * NA_WIN_ROWS - 1
RPB_COLS = 2 * NA_WIN_COLS - 1
NA_KEYS = NA_WIN_ROWS * GRID_W


def _bias_kernel(rep_ref, o_ref, w_scr):
    shape = (GRID_W, RPB_ROWS * GRID_W)
    qc = lax.broadcasted_iota(jnp.int32, shape, 0)
    kc = lax.broadcasted_iota(jnp.int32, shape, 1) % GRID_W
    dc = kc - qc + (NA_WIN_COLS - 1)
    cs = jnp.clip(qc - NA_WIN_COLS // 2, 0, GRID_W - NA_WIN_COLS)
    ok = (kc >= cs) & (kc < cs + NA_WIN_COLS)
    for hh in range(2):
        acc = jnp.zeros(shape, F32)
        for d in range(RPB_COLS):
            acc = jnp.where(dc == d, rep_ref[hh, d:d + 1, :], acc)
        w_scr[...] = jnp.where(ok, acc, NEG_INF)
        for d0 in range(NA_WIN_ROWS):
            o_ref[d0, 0, hh * GRID_W:(hh + 1) * GRID_W, :] = w_scr[:, d0 * GRID_W:d0 * GRID_W + NA_KEYS]


def _bias_slabs(rpb):
    heads = rpb.shape[0]
    rep = jnp.repeat(jnp.transpose(rpb, (0, 2, 1)), GRID_W, axis=-1)
    return pl.pallas_call(
        _bias_kernel,
        grid=(heads // 2,),
        in_specs=[pl.BlockSpec((2, RPB_COLS, RPB_ROWS * GRID_W), lambda p: (p, 0, 0))],
        out_specs=pl.BlockSpec((NA_WIN_ROWS, 1, 2 * GRID_W, NA_KEYS), lambda p: (0, p, 0, 0)),
        out_shape=jax.ShapeDtypeStruct((NA_WIN_ROWS, heads // 2, 2 * GRID_W, NA_KEYS), F32),
        scratch_shapes=[pltpu.VMEM((GRID_W, RPB_ROWS * GRID_W), F32)],
        compiler_params=_cparams(1),
        name="na_bias_slabs",
    )(rep)


def _norm_modulate(x, g, shift, scale):
    y = x * lax.rsqrt(jnp.mean(x * x, axis=-1, keepdims=True) + EPS) * g
    return y * (1.0 + scale) + shift


def _pre_even_kernel(x_ref, mod_ref, g_ref, w_ref, o_ref):
    h = _norm_modulate(x_ref[0], g_ref[...], mod_ref[0, 0:1, :], mod_ref[0, 1:2, :])
    z = jnp.dot(h.astype(BF16), w_ref[...], preferred_element_type=F32)
    o_ref[0] = z.astype(BF16)


def _pre_odd_kernel(x_ref, mod_ref, g_ref, w_ref, cos_ref, sin_ref, o_ref):
    h = _norm_modulate(x_ref[0], g_ref[...], mod_ref[0, 0:1, :], mod_ref[0, 1:2, :])
    z = jnp.dot(h.astype(BF16), w_ref[...], preferred_element_type=F32)
    c = cos_ref[...]
    s = sin_ref[...]

    def rope(t):
        return t * c + pltpu.roll(t, LANES // 2, 1) * s

    for blk in range(SWA_WIDTH // LANES):
        q = rope(z[:, blk * LANES:(blk + 1) * LANES])
        o_ref[0, :, blk * LANES:(blk + 1) * LANES] = q.astype(BF16)
    k0 = SWA_WIDTH
    v0 = k0 + SWA_KV_WIDTH
    b0 = v0 + SWA_KV_WIDTH
    c0 = b0 + CONV_WIDTH
    x0 = c0 + CONV_WIDTH
    o_ref[0, :, 512:1024] = z[:, b0:c0].astype(BF16)
    o_ref[0, :, 1024:1536] = (z[:, c0:x0] * z[:, x0:x0 + CONV_WIDTH]).astype(BF16)
    o_ref[0, :, 1536:1664] = rope(z[:, k0:v0]).astype(BF16)
    o_ref[0, :, 1664:1792] = z[:, v0:b0].astype(BF16)


def _pre(x, mod, g, w, rope=None):
    b, length, d = x.shape
    n_in = w.shape[1]
    tile = min(PRE_TILE, length)
    in_specs = [
        pl.BlockSpec((1, tile, d), lambda i, j: (i, j, 0)),
        pl.BlockSpec((1, 6, d), lambda i, j: (i, 0, 0)),
        _const_spec((1, d)),
        _const_spec((d, n_in)),
    ]
    args = [x, mod, g.reshape(1, d), w]
    if rope is None:
        kern, n_out = _pre_even_kernel, n_in
    else:
        kern, n_out = _pre_odd_kernel, ODD_OUT
        in_specs += [pl.BlockSpec((tile, LANES), lambda i, j: (j, 0))] * 2
        args += list(rope)
    return pl.pallas_call(
        kern,
        grid=(b, length // tile),
        in_specs=in_specs,
        out_specs=pl.BlockSpec((1, tile, n_out), lambda i, j: (i, j, 0)),
        out_shape=jax.ShapeDtypeStruct((b, length, n_out), BF16),
        compiler_params=_cparams(2),
        name="pre_even" if rope is None else "pre_odd",
    )(*args)


def _interleave(main, side):
    out, done = [], 0
    for i, task in enumerate(main):
        out.append(task)
        want = (i + 1) * len(side) // len(main)
        out.extend(side[done:want])
        done = want
    return out


def _pipelined(produce, consume, items, lookahead):
    pending = {}

    def step(idx):
        if idx < len(items):
            pending[idx] = produce(*items[idx])
        if idx >= lookahead:
            done = idx - lookahead
            consume(*items[done], pending.pop(done))

    return [functools.partial(step, idx) for idx in range(len(items) + lookahead)]


def _even_tasks(q_ref, k_ref, v_ref, bias_ref, up_ref, uc_ref, un_ref, pw_ref, ps_ref, out_ref, ext_ref,
                *, j, n_tiles, rows, length):
    row0 = j * ROW_TILE
    win0 = jnp.clip(row0 - NA_WIN_ROWS // 2, 0, rows - NA_KEY_ROWS)
    lane = lax.broadcasted_iota(jnp.int32, (GRID_W, LANES), 1)
    low = lane < HEAD_DIM
    ones_keys = jnp.ones((NA_KEYS, LANES), BF16)

    koffs, d0s = [], []
    for r in range(ROW_TILE):
        row = row0 + r
        rs = jnp.clip(row - NA_WIN_ROWS // 2, 0, rows - NA_WIN_ROWS)
        koffs.append(pl.multiple_of((rs - win0) * GRID_W, GRID_W))
        d0s.append(rs - row + (NA_WIN_ROWS - 1))

    def scores(r, p):
        cols = slice(p * LANES, (p + 1) * LANES)
        qp = q_ref[0, r * GRID_W:(r + 1) * GRID_W, cols]
        zero = jnp.zeros_like(qp)
        qbd = jnp.concatenate([jnp.where(low, qp, zero), jnp.where(low, zero, qp)], axis=0)
        kp = k_ref[0, pl.ds(koffs[r], NA_KEYS), cols]
        s = lax.dot_general(qbd, kp, (((1,), (1,)), ((), ())), preferred_element_type=F32)
        return s + bias_ref[d0s[r], p]

    def attend(r, p, s):
        cols = slice(p * LANES, (p + 1) * LANES)
        m = jnp.max(s, axis=-1, keepdims=True)
        e = jnp.exp(s - m).astype(BF16)
        vp = jnp.concatenate([v_ref[0, pl.ds(koffs[r], NA_KEYS), cols], ones_keys], axis=1)
        oa = jnp.dot(e, vp, preferred_element_type=F32)
        o = oa[:, :LANES] * (1.0 / oa[:, LANES:])
        out_ref[r * GRID_W:(r + 1) * GRID_W, cols] = jnp.where(low, o[:GRID_W], o[GRID_W:]).astype(BF16)

    first = j == 0
    last = j == n_tiles - 1
    zpad = jnp.zeros((POOL_PAD - POOL_EDGE, 512), BF16)
    prev = up_ref[0]
    nxt = un_ref[0]
    ext_ref[0:POOL_PAD - POOL_EDGE, :] = zpad
    ext_ref[POOL_PAD - POOL_EDGE:POOL_PAD, :] = jnp.where(first, jnp.zeros_like(prev), prev)
    ext_ref[POOL_PAD:POOL_PAD + MIX_TILE, :] = uc_ref[0]
    ext_ref[POOL_PAD + MIX_TILE:POOL_PAD + MIX_TILE + POOL_EDGE, :] = jnp.where(last, jnp.zeros_like(nxt), nxt)
    ext_ref[POOL_PAD + MIX_TILE + POOL_EDGE:, :] = zpad
    band_shape = (POOL_BLK, POOL_BLK + 2 * POOL_PAD)
    delta = (lax.broadcasted_iota(jnp.int32, band_shape, 1)
             - lax.broadcasted_iota(jnp.int32, band_shape, 0) - POOL_PAD)
    bands = [jnp.where((delta >= -(w // 2)) & (delta < w - w // 2), 1.0, 0.0).astype(BF16)
             for w in POOL_WINDOWS]

    def pool_sum(blk, g):
        cols = slice(g * POOL_GROUP_DIM, (g + 1) * POOL_GROUP_DIM)
        base = blk * POOL_BLK
        x = ext_ref[base:base + POOL_BLK + 2 * POOL_PAD, cols]
        return jnp.dot(bands[g], x, preferred_element_type=F32)

    def pool_project(blk, g, acc):
        w = POOL_WINDOWS[g]
        cols = slice(g * POOL_GROUP_DIM, (g + 1) * POOL_GROUP_DIM)
        base = blk * POOL_BLK
        u = ext_ref[base + POOL_PAD:base + POOL_PAD + POOL_BLK, cols].astype(F32)
        t = lax.broadcasted_iota(jnp.int32, (POOL_BLK, 1), 0) + (j * MIX_TILE + base)
        lo = jnp.clip(t - w // 2, 0, length)
        hi = jnp.clip(t - w // 2 + w, 0, length)
        mixed = acc / (hi - lo).astype(F32) - u
        y = jnp.dot(mixed.astype(BF16), pw_ref[g], preferred_element_type=F32) * ps_ref[:, cols]
        out_ref[base:base + POOL_BLK, NA_WIDTH + g * POOL_GROUP_DIM:NA_WIDTH + (g + 1) * POOL_GROUP_DIM] = (
            y.astype(BF16))

    units = [(r, p) for r in range(ROW_TILE) for p in range(NA_PAIRS)]
    pools = [(blk, g) for blk in range(MIX_TILE // POOL_BLK) for g in range(len(POOL_WINDOWS))]
    return _interleave(_pipelined(scores, attend, units, NA_LOOKAHEAD),
                       _pipelined(pool_sum, pool_project, pools, POOL_LOOKAHEAD))


SWA_KEYS = 3 * SWA_BLOCK
SWA_WIN = MIX_TILE + 2 * SWA_BLOCK
SWA_GROUP = 4
Q_BLOCKS = MIX_TILE // SWA_BLOCK


def _odd_tasks(q_ref, kv_ref, sink_ref, mask_ref, bg_ref, up_ref, uc_ref, un_ref, cw_ref, cb_ref,
               out_ref, ext_ref, *, j, n_tiles, length):
    t0 = j * MIX_TILE
    win0 = jnp.clip(t0 - SWA_BLOCK, 0, length - SWA_WIN)
    lane = lax.broadcasted_iota(jnp.int32, (SWA_BLOCK, LANES), 1)
    qk_a = (lane // (HEAD_DIM // 2)) % 2 == 0
    v_a = lane < HEAD_DIM
    stack = SWA_GROUP * SWA_BLOCK
    ones_keys = jnp.ones((SWA_KEYS, LANES), BF16)
    koffs, rels = [], []
    for qb in range(Q_BLOCKS):
        qs = t0 + qb * SWA_BLOCK
        ks = jnp.clip(qs - SWA_BLOCK, 0, length - SWA_KEYS)
        koffs.append(pl.multiple_of(ks - win0, SWA_BLOCK))
        rels.append((qs - ks) // SWA_BLOCK)

    def scores(qb, c):
        rows_q = slice(qb * SWA_BLOCK, (qb + 1) * SWA_BLOCK)
        qc = q_ref[0, rows_q, c * LANES:(c + 1) * LANES]
        zero = jnp.zeros_like(qc)
        qst = jnp.concatenate([jnp.where(qk_a, qc, zero), jnp.where(qk_a, zero, qc)], axis=0)
        kp = kv_ref[0, pl.ds(koffs[qb], SWA_KEYS), 0:LANES]
        return lax.dot_general(qst, kp, (((1,), (1,)), ((), ())), preferred_element_type=F32)

    def attend(qb, c, s):
        rows_q = slice(qb * SWA_BLOCK, (qb + 1) * SWA_BLOCK)
        mask = mask_ref[rels[qb]]
        s = jnp.concatenate([s[:SWA_BLOCK] + mask, s[SWA_BLOCK:] + mask], axis=0)
        sink = jnp.concatenate([sink_ref[c * SWA_BLOCK:(c + 1) * SWA_BLOCK],
                                sink_ref[stack + c * SWA_BLOCK:stack + (c + 1) * SWA_BLOCK]], axis=0)
        m = jnp.maximum(jnp.max(s, axis=-1, keepdims=True), sink)
        e = jnp.exp(s - m).astype(BF16)
        vp = jnp.concatenate([kv_ref[0, pl.ds(koffs[qb], SWA_KEYS), LANES:2 * LANES], ones_keys], axis=1)
        oa = jnp.dot(e, vp, preferred_element_type=F32)
        o = oa[:, :LANES] * (1.0 / (oa[:, LANES:] + jnp.exp(sink - m)))
        out_ref[rows_q, c * LANES:(c + 1) * LANES] = jnp.where(v_a, o[:SWA_BLOCK], o[SWA_BLOCK:]).astype(BF16)

    first = j == 0
    last = j == n_tiles - 1
    ext_ref[0:CONV_HALO, :] = jnp.where(first, 0.0, up_ref[0].astype(F32))
    ext_ref[CONV_HALO:CONV_HALO + MIX_TILE, :] = uc_ref[0].astype(F32)
    ext_ref[CONV_HALO + MIX_TILE:, :] = jnp.where(last, 0.0, un_ref[0].astype(F32))

    def conv_chunk(blk):
        rows_c = slice(blk * SWA_BLOCK, (blk + 1) * SWA_BLOCK)
        conv = cb_ref[...]
        for jj in range(3):
            off = CONV_HALO - 1 + jj + blk * SWA_BLOCK
            conv = conv + ext_ref[off:off + SWA_BLOCK, :] * cw_ref[jj:jj + 1, :]
        out_ref[rows_c, SWA_WIDTH:SWA_WIDTH + CONV_WIDTH] = (
            bg_ref[0, rows_c, :].astype(F32) * conv).astype(BF16)

    units = [(qb, c) for qb in range(Q_BLOCKS) for c in range(SWA_GROUP)]
    return _interleave(_pipelined(scores, attend, units, SWA_LOOKAHEAD),
                       [functools.partial(conv_chunk, blk) for blk in range(Q_BLOCKS)])


def _swa_masks():
    q = np.arange(SWA_BLOCK)[:, None]
    k = np.arange(SWA_KEYS)[None, :]
    slabs = [np.where(np.abs(q + rel * SWA_BLOCK - k) <= SWA_WINDOW, 0.0, -np.inf) for rel in range(3)]
    return jnp.asarray(np.stack(slabs), F32)


POST_CHUNK = 256


def _post_tasks(x_ref, mix_ref, mod_ref, g_ref, wo_ref, w1_ref, w3_ref, w2_ref, fg_ref, o_ref, hid_ref,
                *, final):
    st = {"x1": [], "x2": []}
    n_d = D_MODEL // POST_CHUNK
    n_ff = D_FF // POST_CHUNK

    def proj_out(c):
        cols = slice(c * POST_CHUNK, (c + 1) * POST_CHUNK)
        mix = jnp.dot(mix_ref[...], wo_ref[:, cols], preferred_element_type=F32)
        st["x1"].append(x_ref[0, :, cols] + mod_ref[0, 2:3, cols] * mix)

    def norm():
        x1 = jnp.concatenate(st["x1"], axis=1)
        st["x1_full"] = x1
        st["h"] = _norm_modulate(x1, g_ref[...], mod_ref[0, 3:4, :], mod_ref[0, 4:5, :]).astype(BF16)

    def up(c):
        cols = slice(c * POST_CHUNK, (c + 1) * POST_CHUNK)
        a = jnp.dot(st["h"], w1_ref[:, cols], preferred_element_type=F32)
        b = jnp.dot(st["h"], w3_ref[:, cols], preferred_element_type=F32)
        hid_ref[:, cols] = (a * jax.nn.sigmoid(a) * b).astype(BF16)

    def down(c):
        cols = slice(c * POST_CHUNK, (c + 1) * POST_CHUNK)
        f = jnp.dot(hid_ref[...], w2_ref[:, cols], preferred_element_type=F32)
        x2 = st["x1_full"][:, cols] + mod_ref[0, 5:6, cols] * f
        if final:
            st["x2"].append(x2)
        else:
            o_ref[0, :, cols] = x2

    def final_norm():
        x2 = jnp.concatenate(st["x2"], axis=1)
        o_ref[0] = x2 * lax.rsqrt(jnp.mean(x2 * x2, axis=-1, keepdims=True) + EPS) * fg_ref[...]

    tasks = [functools.partial(proj_out, c) for c in range(n_d)] + [norm]
    tasks += [functools.partial(up, c) for c in range(n_ff)]
    tasks += [functools.partial(down, c) for c in range(n_d)]
    if final:
        tasks.append(final_norm)
    return tasks


def _mixpost_kernel(*refs, n_mix, mixer, final, n_tiles, mixer_kw):
    mix_refs = refs[:n_mix]
    (x_ref, mod_ref, g_ref, wo_ref, w1_ref, w3_ref, w2_ref, fg_ref, o_ref,
     mix_new, mix_old, hid_ref, ext_ref) = refs[n_mix:]
    s = pl.program_id(0)
    n_steps = pl.num_programs(0)

    @pl.when(s == 0)
    def _():
        mix_new[...] = jnp.zeros_like(mix_new)

    mix_old[...] = mix_new[...]
    j = jnp.minimum(s, n_steps - 2) % n_tiles
    mixer_tasks = mixer(*mix_refs, mix_new, ext_ref, j=j, n_tiles=n_tiles, **mixer_kw)
    post_tasks = _post_tasks(x_ref, mix_old, mod_ref, g_ref, wo_ref, w1_ref, w3_ref, w2_ref, fg_ref, o_ref,
                             hid_ref, final=final)
    for task in _interleave(post_tasks, mixer_tasks):
        task()


def _mixpost(kind, z, mixer_consts, x, mod, g, wo, w1, w3, w2, final_g, final):
    b, length, d = x.shape
    assert length % MIX_TILE == 0
    n_tiles = length // MIX_TILE
    total = b * n_tiles

    def mix_tile(s):
        t = jnp.minimum(s, total - 1)
        return t // n_tiles, t % n_tiles

    def post_tile(s):
        t = jnp.maximum(s - 1, 0)
        return t // n_tiles, t % n_tiles

    def tile_spec(width, col_block):
        return pl.BlockSpec((1, MIX_TILE, width), lambda s: (*mix_tile(s), col_block))

    def halo_specs(rows, col_block):
        per_tile = MIX_TILE // rows
        n_halo = length // rows

        def before(s):
            i, j = mix_tile(s)
            return i, jnp.maximum(j * per_tile - 1, 0), col_block

        def after(s):
            i, j = mix_tile(s)
            return i, jnp.minimum((j + 1) * per_tile, n_halo - 1), col_block

        return (pl.BlockSpec((1, rows, 512), before), pl.BlockSpec((1, rows, 512), after))

    if kind == "even":
        rows = length // GRID_W
        assert rows >= NA_KEY_ROWS
        bias, pool_w, pool_scale = mixer_consts

        def win_map(s, col0):
            i, j = mix_tile(s)
            win0 = jnp.clip(j * ROW_TILE - NA_WIN_ROWS // 2, 0, rows - NA_KEY_ROWS)
            return i, pl.multiple_of(win0 * GRID_W, GRID_W), col0

        win = (pl.Element(1), pl.Element(NA_KEY_ROWS * GRID_W), pl.Element(NA_WIDTH))
        u_before, u_after = halo_specs(POOL_EDGE, 3)
        mix_specs = [
            tile_spec(NA_WIDTH, 0),
            pl.BlockSpec(win, lambda s: win_map(s, NA_WIDTH)),
            pl.BlockSpec(win, lambda s: win_map(s, 2 * NA_WIDTH)),
            _const_spec(bias.shape),
            u_before, tile_spec(512, 3), u_after,
            _const_spec(pool_w.shape),
            _const_spec((1, 512)),
        ]
        mix_args = [z, z, z, bias, z, z, z, pool_w, pool_scale.reshape(1, 512)]
        mixer, mixer_kw = _even_tasks, dict(rows=rows, length=length)
        ext = pltpu.VMEM((MIX_TILE + 2 * POOL_PAD, 512), BF16)
    else:
        assert length >= SWA_WIN
        sink_rows, conv_w, conv_b = mixer_consts

        def win_map(s):
            i, j = mix_tile(s)
            start = jnp.clip(j * MIX_TILE - SWA_BLOCK, 0, length - SWA_WIN)
            return i, pl.multiple_of(start, SWA_BLOCK), ODD_OUT - 2 * SWA_KV_WIDTH

        u_before, u_after = halo_specs(CONV_HALO, 2)
        mix_specs = [
            tile_spec(512, 0),
            pl.BlockSpec((pl.Element(1), pl.Element(SWA_WIN), pl.Element(2 * SWA_KV_WIDTH)), win_map),
            _const_spec((2 * SWA_GROUP * SWA_BLOCK, 1)),
            _const_spec((3, SWA_BLOCK, SWA_KEYS)),
            tile_spec(512, 1),
            u_before, tile_spec(512, 2), u_after,
            _const_spec((3, CONV_WIDTH)),
            _const_spec((1, CONV_WIDTH)),
        ]
        mix_args = [z, z, sink_rows, _swa_masks(), z, z, z, z, conv_w, conv_b.reshape(1, CONV_WIDTH)]
        mixer, mixer_kw = _odd_tasks, dict(length=length)
        ext = pltpu.VMEM((MIX_TILE + 2 * CONV_HALO, 512), F32)

    post_specs = [
        pl.BlockSpec((1, MIX_TILE, d), lambda s: (*post_tile(s), 0)),
        pl.BlockSpec((1, 6, d), lambda s: (post_tile(s)[0], 0, 0)),
        _const_spec((1, d)),
        _const_spec(wo.shape),
        _const_spec(w1.shape),
        _const_spec(w3.shape),
        _const_spec(w2.shape),
        _const_spec((1, d)),
    ]
    post_args = [x, mod, g.reshape(1, d), wo, w1, w3, w2, final_g.reshape(1, d)]
    return pl.pallas_call(
        functools.partial(_mixpost_kernel, n_mix=len(mix_specs), mixer=mixer, final=final, n_tiles=n_tiles,
                          mixer_kw=mixer_kw),
        grid=(total + 1,),
        in_specs=mix_specs + post_specs,
        out_specs=pl.BlockSpec((1, MIX_TILE, d), lambda s: (*post_tile(s), 0)),
        out_shape=jax.ShapeDtypeStruct((b, length, d), F32),
        scratch_shapes=[pltpu.VMEM((MIX_TILE, D_MODEL), BF16), pltpu.VMEM((MIX_TILE, D_MODEL), BF16),
                        pltpu.VMEM((MIX_TILE, D_FF), BF16), ext],
        compiler_params=_cparams(1),
        name="mixpost_" + kind,
    )(*mix_args, *post_args)


def _odd_in_perm():
    half = HEAD_DIM // 2
    q_cols = []
    for c in range(SWA_GROUP):
        for seg in range(4):
            head = c if seg % 2 == 0 else SWA_GROUP + c
            base = head * HEAD_DIM + (half if seg >= 2 else 0)
            q_cols += list(range(base, base + half))
    k_cols = []
    for seg in range(4):
        base = SWA_WIDTH + (seg % 2) * HEAD_DIM + (half if seg >= 2 else 0)
        k_cols += list(range(base, base + half))
    rest = list(range(SWA_WIDTH + SWA_KV_WIDTH, ODD_IN))
    return np.asarray(q_cols + k_cols + rest, dtype=np.int32)


def _odd_out_perm():
    rows = []
    for c in range(SWA_GROUP):
        for head in (c, SWA_GROUP + c):
            rows += list(range(head * HEAD_DIM, (head + 1) * HEAD_DIM))
    rows += list(range(SWA_WIDTH, D_MODEL))
    return np.asarray(rows, dtype=np.int32)


def _fold_qk_scale(w_in, q_width):
    scale = HEAD_DIM ** -0.5
    assert np.log2(scale) == round(np.log2(scale))
    col = jnp.arange(w_in.shape[-1]) < q_width
    return w_in * jnp.where(col, scale, 1.0).astype(w_in.dtype)


def _sink_rows(sink):
    return jnp.repeat(sink.astype(F32), SWA_BLOCK).reshape(2 * SWA_GROUP * SWA_BLOCK, 1)


def _trunk(x, mods, p):
    depth = p["ffn_w1"].shape[0]
    for i in range(depth):
        jdx = i // 2
        mod = mods[i]
        if i % 2 == 0:
            kind = "even"
            z = _pre(x, mod, p["norm_g"][i, 0], p["even_w_in"][jdx])
            consts = (p["na_bias"][jdx], p["pool_w"][jdx], p["pool_scale"][jdx])
            wo = p["even_w_out"][jdx]
        else:
            kind = "odd"
            z = _pre(x, mod, p["norm_g"][i, 0], p["odd_w_in"][jdx], rope=p["rope"])
            consts = (p["sink_rows"][jdx], p["conv_w"][jdx], p["conv_b"][jdx])
            wo = p["odd_w_out"][jdx]
        x = _mixpost(kind, z, consts, x, mod, p["norm_g"][i, 1], wo, p["ffn_w1"][i], p["ffn_w3"][i],
                     p["ffn_w2"][i], p["final_g"], final=(i == depth - 1))
    return x


def kernel(x_prompt, x_sample, c_prompt, c_sample, ada_w, ada_b, norm_g, final_g, ffn_w1, ffn_w3, ffn_w2,
           even_w_in, na_rpb, pool_w, pool_scale, even_w_out, odd_w_in, swa_sink, conv_w, conv_b, odd_w_out):
    depth = ada_w.shape[0]
    nb_p, nb_s = c_prompt.shape[0], c_sample.shape[0]
    pad = (-(nb_p + nb_s)) % 8
    c_all = jnp.concatenate([c_prompt, c_sample, jnp.zeros((pad, D_MODEL), F32)], axis=0)
    mod = _ada_mod(c_all, ada_w, ada_b)
    mods_p = [mod[i, :nb_p].reshape(nb_p, 6, D_MODEL) for i in range(depth)]
    mods_s = [mod[i, nb_p:nb_p + nb_s].reshape(nb_s, 6, D_MODEL) for i in range(depth)]

    max_len = max(x_prompt.shape[1], x_sample.shape[1])
    p = {
        "norm_g": norm_g, "final_g": final_g,
        "ffn_w1": ffn_w1.astype(BF16), "ffn_w3": ffn_w3.astype(BF16), "ffn_w2": ffn_w2.astype(BF16),
        "even_w_in": _fold_qk_scale(even_w_in, NA_WIDTH).astype(BF16),
        "even_w_out": even_w_out.astype(BF16),
        "na_bias": [_bias_slabs(na_rpb[j]) for j in range(na_rpb.shape[0])],
        "pool_w": pool_w.astype(BF16), "pool_scale": pool_scale,
        "odd_w_in": _fold_qk_scale(odd_w_in, SWA_WIDTH)[:, :, _odd_in_perm()].astype(BF16),
        "odd_w_out": odd_w_out[:, _odd_out_perm(), :].astype(BF16),
        "sink_rows": [_sink_rows(swa_sink[j]) for j in range(swa_sink.shape[0])],
        "conv_w": conv_w, "conv_b": conv_b,
        "rope": _rope_tables(max_len),
    }
    return (_trunk(x_prompt, mods_p, p), _trunk(x_sample, mods_s, p))
```

```python
import functools

import numpy as np
import jax
import jax.numpy as jnp
from jax import lax
from jax.experimental import pallas as pl
from jax.experimental.pallas import tpu as pltpu

F32 = jnp.float32
BF16 = jnp.bfloat16

D_MODEL = 1024
HEAD_DIM = 64
LANES = 128
GRID_W = 64
NA_WIN_ROWS = 8
NA_WIN_COLS = 16
NA_WIDTH = 512
NA_PAIRS = NA_WIDTH // LANES
POOL_WINDOWS = (2, 4, 8, 16)
POOL_GROUP_DIM = 128
POOL_BLK = 128
POOL_PAD = 64
POOL_EDGE = 16
CONV_HALO = 8
EVEN_IN = 2048
SWA_WIDTH = 512
SWA_KV_WIDTH = 128
SWA_WINDOW = 128
SWA_BLOCK = 128
CONV_WIDTH = 512
ODD_IN = 2304
ODD_OUT = 1792
D_FF = 2816
ROPE_THETA = 10000.0
EPS = 1e-6
NEG_INF = float("-inf")

ROW_TILE = 8
MIX_TILE = ROW_TILE * GRID_W
NA_KEY_ROWS = ROW_TILE + NA_WIN_ROWS - 1
NA_LOOKAHEAD = 3
SWA_LOOKAHEAD = 2
POOL_LOOKAHEAD = 2
PRE_TILE = 1024
VMEM_LIMIT = 56 * 1024 * 1024


def _cparams(n_axes):
    return pltpu.CompilerParams(
        dimension_semantics=("arbitrary",) * n_axes,
        vmem_limit_bytes=VMEM_LIMIT,
    )


def _const_spec(shape):
    nd = len(shape)
    return pl.BlockSpec(shape, lambda *_: (0,) * nd, pipeline_mode=pl.Buffered(1))


ADA_TILE = 1536


def _ada_kernel(c_ref, w_ref, b_ref, o_ref):
    c = c_ref[...]
    act = c * jax.nn.sigmoid(c)
    o_ref[0] = jnp.dot(act, w_ref[0], precision=lax.Precision.HIGHEST,
                       preferred_element_type=F32) + b_ref[0]


def _ada_mod(c_all, ada_w, ada_b):
    depth, d, n = ada_w.shape
    rows = c_all.shape[0]
    return pl.pallas_call(
        _ada_kernel,
        grid=(depth, n // ADA_TILE),
        in_specs=[
            pl.BlockSpec((rows, d), lambda l, j: (0, 0)),
            pl.BlockSpec((1, d, ADA_TILE), lambda l, j: (l, 0, j)),
            pl.BlockSpec((1, 1, ADA_TILE), lambda l, j: (l, 0, j)),
        ],
        out_specs=pl.BlockSpec((1, rows, ADA_TILE), lambda l, j: (l, 0, j)),
        out_shape=jax.ShapeDtypeStruct((depth, rows, n), F32),
        compiler_params=_cparams(2),
        name="ada_mod",
    )(c_all, ada_w, ada_b.reshape(depth, 1, n))


ROPE_TILE = 2048


def _rope_kernel(inv_ref, cos_ref, sin_ref):
    tile = cos_ref.shape[0]
    i = pl.program_id(0)
    pos = (lax.broadcasted_iota(jnp.int32, (tile, LANES), 0) + i * tile).astype(F32)
    ang = pos * inv_ref[...]
    lane = lax.broadcasted_iota(jnp.int32, (tile, LANES), 1)
    s = jnp.sin(ang)
    cos_ref[...] = jnp.cos(ang)
    sin_ref[...] = jnp.where(lane < LANES // 2, -s, s)


def _rope_tables(length):
    inv = 1.0 / (ROPE_THETA ** (jnp.arange(0, HEAD_DIM, 2, dtype=F32) / HEAD_DIM))
    inv_lanes = jnp.tile(inv, LANES // (HEAD_DIM // 2)).reshape(1, LANES)
    tile = min(ROPE_TILE, length)
    return pl.pallas_call(
        _rope_kernel,
        grid=(length // tile,),
        in_specs=[pl.BlockSpec((1, LANES), lambda i: (0, 0))],
        out_specs=[pl.BlockSpec((tile, LANES), lambda i: (i, 0))] * 2,
        out_shape=[jax.ShapeDtypeStruct((length, LANES), F32)] * 2,
        compiler_params=_cparams(1),
        name="rope_tables",
    )(inv_lanes)


RPB_ROWS = 2 * NA_WIN_ROWS - 1
RPB_COLS = 2 * NA_WIN_COLS - 1
NA_KEYS = NA_WIN_ROWS * GRID_W


def _bias_kernel(rep_ref, o_ref, w_scr):
    shape = (GRID_W, RPB_ROWS * GRID_W)
    qc = lax.broadcasted_iota(jnp.int32, shape, 0)
    kc = lax.broadcasted_iota(jnp.int32, shape, 1) % GRID_W
    dc = kc - qc + (NA_WIN_COLS - 1)
    cs = jnp.clip(qc - NA_WIN_COLS // 2, 0, GRID_W - NA_WIN_COLS)
    ok = (kc >= cs) & (kc < cs + NA_WIN_COLS)
    for hh in range(2):
        acc = jnp.zeros(shape, F32)
        for d in range(RPB_COLS):
            acc = jnp.where(dc == d, rep_ref[hh, d:d + 1, :], acc)
        w_scr[...] = jnp.where(ok, acc, NEG_INF)
        for d0 in range(NA_WIN_ROWS):
            o_ref[d0, 0, hh * GRID_W:(hh + 1) * GRID_W, :] = w_scr[:, d0 * GRID_W:d0 * GRID_W + NA_KEYS]


def _bias_slabs(rpb):
    heads = rpb.shape[0]
    rep = jnp.repeat(jnp.transpose(rpb, (0, 2, 1)), GRID_W, axis=-1)
    return pl.pallas_call(
        _bias_kernel,
        grid=(heads // 2,),
        in_specs=[pl.BlockSpec((2, RPB_COLS, RPB_ROWS * GRID_W), lambda p: (p, 0, 0))],
        out_specs=pl.BlockSpec((NA_WIN_ROWS, 1, 2 * GRID_W, NA_KEYS), lambda p: (0, p, 0, 0)),
        out_shape=jax.ShapeDtypeStruct((NA_WIN_ROWS, heads // 2, 2 * GRID_W, NA_KEYS), F32),
        scratch_shapes=[pltpu.VMEM((GRID_W, RPB_ROWS * GRID_W), F32)],
        compiler_params=_cparams(1),
        name="na_bias_slabs",
    )(rep)


def _norm_modulate(x, g, shift, scale):
    y = x * lax.rsqrt(jnp.mean(x * x, axis=-1, keepdims=True) + EPS) * g
    return y * (1.0 + scale) + shift


def _pre_even_kernel(x_ref, mod_ref, g_ref, w_ref, o_ref):
    h = _norm_modulate(x_ref[0], g_ref[...], mod_ref[0, 0:1, :], mod_ref[0, 1:2, :])
    z = jnp.dot(h.astype(BF16), w_ref[...], preferred_element_type=F32)
    o_ref[0] = z.astype(BF16)


def _pre_odd_kernel(x_ref, mod_ref, g_ref, w_ref, cos_ref, sin_ref, o_ref):
    h = _norm_modulate(x_ref[0], g_ref[...], mod_ref[0, 0:1, :], mod_ref[0, 1:2, :])
    z = jnp.dot(h.astype(BF16), w_ref[...], preferred_element_type=F32)
    c = cos_ref[...]
    s = sin_ref[...]

    def rope(t):
        return t * c + pltpu.roll(t, LANES // 2, 1) * s

    for blk in range(SWA_WIDTH // LANES):
        q = rope(z[:, blk * LANES:(blk + 1) * LANES])
        o_ref[0, :, blk * LANES:(blk + 1) * LANES] = q.astype(BF16)
    k0 = SWA_WIDTH
    v0 = k0 + SWA_KV_WIDTH
    b0 = v0 + SWA_KV_WIDTH
    c0 = b0 + CONV_WIDTH
    x0 = c0 + CONV_WIDTH
    o_ref[0, :, 512:1024] = z[:, b0:c0].astype(BF16)
    o_ref[0, :, 1024:1536] = (z[:, c0:x0] * z[:, x0:x0 + CONV_WIDTH]).astype(BF16)
    o_ref[0, :, 1536:1664] = rope(z[:, k0:v0]).astype(BF16)
    o_ref[0, :, 1664:1792] = z[:, v0:b0].astype(BF16)


def _pre(x, mod, g, w, rope=None):
    b, length, d = x.shape
    n_in = w.shape[1]
    tile = min(PRE_TILE, length)
    in_specs = [
        pl.BlockSpec((1, tile, d), lambda i, j: (i, j, 0)),
        pl.BlockSpec((1, 6, d), lambda i, j: (i, 0, 0)),
        _const_spec((1, d)),
        _const_spec((d, n_in)),
    ]
    args = [x, mod, g.reshape(1, d), w]
    if rope is None:
        kern, n_out = _pre_even_kernel, n_in
    else:
        kern, n_out = _pre_odd_kernel, ODD_OUT
        in_specs += [pl.BlockSpec((tile, LANES), lambda i, j: (j, 0))] * 2
        args += list(rope)
    return pl.pallas_call(
        kern,
        grid=(b, length // tile),
        in_specs=in_specs,
        out_specs=pl.BlockSpec((1, tile, n_out), lambda i, j: (i, j, 0)),
        out_shape=jax.ShapeDtypeStruct((b, length, n_out), BF16),
        compiler_params=_cparams(2),
        name="pre_even" if rope is None else "pre_odd",
    )(*args)


def _interleave(main, side):
    out, done = [], 0
    for i, task in enumerate(main):
        out.append(task)
        want = (i + 1) * len(side) // len(main)
        out.extend(side[done:want])
        done = want
    return out


def _pipelined(produce, consume, items, lookahead):
    pending = {}

    def step(idx):
        if idx < len(items):
            pending[idx] = produce(*items[idx])
        if idx >= lookahead:
            done = idx - lookahead
            consume(*items[done], pending.pop(done))

    return [functools.partial(step, idx) for idx in range(len(items) + lookahead)]


def _even_tasks(q_ref, k_ref, v_ref, bias_ref, up_ref, uc_ref, un_ref, pw_ref, ps_ref, out_ref, ext_ref,
                *, j, n_tiles, rows, length):
    row0 = j * ROW_TILE
    win0 = jnp.clip(row0 - NA_WIN_ROWS // 2, 0, rows - NA_KEY_ROWS)
    lane = lax.broadcasted_iota(jnp.int32, (GRID_W, LANES), 1)
    low = lane < HEAD_DIM
    ones_keys = jnp.ones((NA_KEYS, LANES), BF16)

    koffs, d0s = [], []
    for r in range(ROW_TILE):
        row = row0 + r
        rs = jnp.clip(row - NA_WIN_ROWS // 2, 0, rows - NA_WIN_ROWS)
        koffs.append(pl.multiple_of((rs - win0) * GRID_W, GRID_W))
        d0s.append(rs - row + (NA_WIN_ROWS - 1))

    def scores(r, p):
        cols = slice(p * LANES, (p + 1) * LANES)
        qp = q_ref[0, r * GRID_W:(r + 1) * GRID_W, cols]
        zero = jnp.zeros_like(qp)
        qbd = jnp.concatenate([jnp.where(low, qp, zero), jnp.where(low, zero, qp)], axis=0)
        kp = k_ref[0, pl.ds(koffs[r], NA_KEYS), cols]
        s = lax.dot_general(qbd, kp, (((1,), (1,)), ((), ())), preferred_element_type=F32)
        return s + bias_ref[d0s[r], p]

    def attend(r, p, s):
        cols = slice(p * LANES, (p + 1) * LANES)
        m = jnp.max(s, axis=-1, keepdims=True)
        e = jnp.exp(s - m).astype(BF16)
        vp = jnp.concatenate([v_ref[0, pl.ds(koffs[r], NA_KEYS), cols], ones_keys], axis=1)
        oa = jnp.dot(e, vp, preferred_element_type=F32)
        o = oa[:, :LANES] * (1.0 / oa[:, LANES:])
        out_ref[r * GRID_W:(r + 1) * GRID_W, cols] = jnp.where(low, o[:GRID_W], o[GRID_W:]).astype(BF16)

    first = j == 0
    last = j == n_tiles - 1
    zpad = jnp.zeros((POOL_PAD - POOL_EDGE, 512), BF16)
    prev = up_ref[0]
    nxt = un_ref[0]
    ext_ref[0:POOL_PAD - POOL_EDGE, :] = zpad
    ext_ref[POOL_PAD - POOL_EDGE:POOL_PAD, :] = jnp.where(first, jnp.zeros_like(prev), prev)
    ext_ref[POOL_PAD:POOL_PAD + MIX_TILE, :] = uc_ref[0]
    ext_ref[POOL_PAD + MIX_TILE:POOL_PAD + MIX_TILE + POOL_EDGE, :] = jnp.where(last, jnp.zeros_like(nxt), nxt)
    ext_ref[POOL_PAD + MIX_TILE + POOL_EDGE:, :] = zpad
    band_shape = (POOL_BLK, POOL_BLK + 2 * POOL_PAD)
    delta = (lax.broadcasted_iota(jnp.int32, band_shape, 1)
             - lax.broadcasted_iota(jnp.int32, band_shape, 0) - POOL_PAD)
    bands = [jnp.where((delta >= -(w // 2)) & (delta < w - w // 2), 1.0, 0.0).astype(BF16)
             for w in POOL_WINDOWS]

    def pool_sum(blk, g):
        cols = slice(g * POOL_GROUP_DIM, (g + 1) * POOL_GROUP_DIM)
        base = blk * POOL_BLK
        x = ext_ref[base:base + POOL_BLK + 2 * POOL_PAD, cols]
        return jnp.dot(bands[g], x, preferred_element_type=F32)

    def pool_project(blk, g, acc):
        w = POOL_WINDOWS[g]
        cols = slice(g * POOL_GROUP_DIM, (g + 1) * POOL_GROUP_DIM)
        base = blk * POOL_BLK
        u = ext_ref[base + POOL_PAD:base + POOL_PAD + POOL_BLK, cols].astype(F32)
        t = lax.broadcasted_iota(jnp.int32, (POOL_BLK, 1), 0) + (j * MIX_TILE + base)
        lo = jnp.clip(t - w // 2, 0, length)
        hi = jnp.clip(t - w // 2 + w, 0, length)
        mixed = acc / (hi - lo).astype(F32) - u
        y = jnp.dot(mixed.astype(BF16), pw_ref[g], preferred_element_type=F32) * ps_ref[:, cols]
        out_ref[base:base + POOL_BLK, NA_WIDTH + g * POOL_GROUP_DIM:NA_WIDTH + (g + 1) * POOL_GROUP_DIM] = (
            y.astype(BF16))

    units = [(r, p) for r in range(ROW_TILE) for p in range(NA_PAIRS)]
    pools = [(blk, g) for blk in range(MIX_TILE // POOL_BLK) for g in range(len(POOL_WINDOWS))]
    return _interleave(_pipelined(scores, attend, units, NA_LOOKAHEAD),
                       _pipelined(pool_sum, pool_project, pools, POOL_LOOKAHEAD))


SWA_KEYS = 3 * SWA_BLOCK
SWA_WIN = MIX_TILE + 2 * SWA_BLOCK
SWA_GROUP = 4
Q_BLOCKS = MIX_TILE // SWA_BLOCK


def _odd_tasks(q_ref, kv_ref, sink_ref, mask_ref, bg_ref, up_ref, uc_ref, un_ref, cw_ref, cb_ref,
               out_ref, ext_ref, *, j, n_tiles, length):
    t0 = j * MIX_TILE
    win0 = jnp.clip(t0 - SWA_BLOCK, 0, length - SWA_WIN)
    lane = lax.broadcasted_iota(jnp.int32, (SWA_BLOCK, LANES), 1)
    qk_a = (lane // (HEAD_DIM // 2)) % 2 == 0
    v_a = lane < HEAD_DIM
    stack = SWA_GROUP * SWA_BLOCK
    ones_keys = jnp.ones((SWA_KEYS, LANES), BF16)
    koffs, rels = [], []
    for qb in range(Q_BLOCKS):
        qs = t0 + qb * SWA_BLOCK
        ks = jnp.clip(qs - SWA_BLOCK, 0, length - SWA_KEYS)
        koffs.append(pl.multiple_of(ks - win0, SWA_BLOCK))
        rels.append((qs - ks) // SWA_BLOCK)

    def scores(qb, c):
        rows_q = slice(qb * SWA_BLOCK, (qb + 1) * SWA_BLOCK)
        qc = q_ref[0, rows_q, c * LANES:(c + 1) * LANES]
        zero = jnp.zeros_like(qc)
        qst = jnp.concatenate([jnp.where(qk_a, qc, zero), jnp.where(qk_a, zero, qc)], axis=0)
        kp = kv_ref[0, pl.ds(koffs[qb], SWA_KEYS), 0:LANES]
        return lax.dot_general(qst, kp, (((1,), (1,)), ((), ())), preferred_element_type=F32)

    def attend(qb, c, s):
        rows_q = slice(qb * SWA_BLOCK, (qb + 1) * SWA_BLOCK)
        mask = mask_ref[rels[qb]]
        s = jnp.concatenate([s[:SWA_BLOCK] + mask, s[SWA_BLOCK:] + mask], axis=0)
        sink = jnp.concatenate([sink_ref[c * SWA_BLOCK:(c + 1) * SWA_BLOCK],
                                sink_ref[stack + c * SWA_BLOCK:stack + (c + 1) * SWA_BLOCK]], axis=0)
        m = jnp.maximum(jnp.max(s, axis=-1, keepdims=True), sink)
        e = jnp.exp(s - m).astype(BF16)
        vp = jnp.concatenate([kv_ref[0, pl.ds(koffs[qb], SWA_KEYS), LANES:2 * LANES], ones_keys], axis=1)
        oa = jnp.dot(e, vp, preferred_element_type=F32)
        o = oa[:, :LANES] * (1.0 / (oa[:, LANES:] + jnp.exp(sink - m)))
        out_ref[rows_q, c * LANES:(c + 1) * LANES] = jnp.where(v_a, o[:SWA_BLOCK], o[SWA_BLOCK:]).astype(BF16)

    first = j == 0
    last = j == n_tiles - 1
    ext_ref[0:CONV_HALO, :] = jnp.where(first, 0.0, up_ref[0].astype(F32))
    ext_ref[CONV_HALO:CONV_HALO + MIX_TILE, :] = uc_ref[0].astype(F32)
    ext_ref[CONV_HALO + MIX_TILE:, :] = jnp.where(last, 0.0, un_ref[0].astype(F32))

    def conv_chunk(blk):
        rows_c = slice(blk * SWA_BLOCK, (blk + 1) * SWA_BLOCK)
        conv = cb_ref[...]
        for jj in range(3):
            off = CONV_HALO - 1 + jj + blk * SWA_BLOCK
            conv = conv + ext_ref[off:off + SWA_BLOCK, :] * cw_ref[jj:jj + 1, :]
        out_ref[rows_c, SWA_WIDTH:SWA_WIDTH + CONV_WIDTH] = (
            bg_ref[0, rows_c, :].astype(F32) * conv).astype(BF16)

    units = [(qb, c) for qb in range(Q_BLOCKS) for c in range(SWA_GROUP)]
    return _interleave(_pipelined(scores, attend, units, SWA_LOOKAHEAD),
                       [functools.partial(conv_chunk, blk) for blk in range(Q_BLOCKS)])


def _swa_masks():
    q = np.arange(SWA_BLOCK)[:, None]
    k = np.arange(SWA_KEYS)[None, :]
    slabs = [np.where(np.abs(q + rel * SWA_BLOCK - k) <= SWA_WINDOW, 0.0, -np.inf) for rel in range(3)]
    return jnp.asarray(np.stack(slabs), F32)


POST_CHUNK = 256


def _post_tasks(x_ref, mix_ref, mod_ref, g_ref, wo_ref, w1_ref, w3_ref, w2_ref, fg_ref, o_ref, hid_ref,
                *, final):
    st = {"x1": [], "x2": []}
    n_d = D_MODEL // POST_CHUNK
    n_ff = D_FF // POST_CHUNK

    def proj_out(c):
        cols = slice(c * POST_CHUNK, (c + 1) * POST_CHUNK)
        mix = jnp.dot(mix_ref[...], wo_ref[:, cols], preferred_element_type=F32)
        st["x1"].append(x_ref[0, :, cols] + mod_ref[0, 2:3, cols] * mix)

    def norm():
        x1 = jnp.concatenate(st["x1"], axis=1)
        st["x1_full"] = x1
        st["h"] = _norm_modulate(x1, g_ref[...], mod_ref[0, 3:4, :], mod_ref[0, 4:5, :]).astype(BF16)

    def up(c):
        cols = slice(c * POST_CHUNK, (c + 1) * POST_CHUNK)
        a = jnp.dot(st["h"], w1_ref[:, cols], preferred_element_type=F32)
        b = jnp.dot(st["h"], w3_ref[:, cols], preferred_element_type=F32)
        hid_ref[:, cols] = (a * jax.nn.sigmoid(a) * b).astype(BF16)

    def down(c):
        cols = slice(c * POST_CHUNK, (c + 1) * POST_CHUNK)
        f = jnp.dot(hid_ref[...], w2_ref[:, cols], preferred_element_type=F32)
        x2 = st["x1_full"][:, cols] + mod_ref[0, 5:6, cols] * f
        if final:
            st["x2"].append(x2)
        else:
            o_ref[0, :, cols] = x2

    def final_norm():
        x2 = jnp.concatenate(st["x2"], axis=1)
        o_ref[0] = x2 * lax.rsqrt(jnp.mean(x2 * x2, axis=-1, keepdims=True) + EPS) * fg_ref[...]

    tasks = [functools.partial(proj_out, c) for c in range(n_d)] + [norm]
    tasks += [functools.partial(up, c) for c in range(n_ff)]
    tasks += [functools.partial(down, c) for c in range(n_d)]
    if final:
        tasks.append(final_norm)
    return tasks


def _mixpost_kernel(*refs, n_mix, mixer, final, n_tiles, mixer_kw):
    mix_refs = refs[:n_mix]
    (x_ref, mod_ref, g_ref, wo_ref, w1_ref, w3_ref, w2_ref, fg_ref, o_ref,
     mix_new, mix_old, hid_ref, ext_ref) = refs[n_mix:]
    s = pl.program_id(0)
    last = pl.num_programs(0) - 1

    def mixer_tasks():
        return mixer(*mix_refs, mix_new, ext_ref, j=s % n_tiles, n_tiles=n_tiles, **mixer_kw)

    def post_tasks():
        return _post_tasks(x_ref, mix_old, mod_ref, g_ref, wo_ref, w1_ref, w3_ref, w2_ref, fg_ref, o_ref,
                           hid_ref, final=final)

    @pl.when(s == 0)
    def _():
        for task in mixer_tasks():
            task()

    @pl.when(jnp.logical_and(s > 0, s < last))
    def _():
        mix_old[...] = mix_new[...]
        for task in _interleave(post_tasks(), mixer_tasks()):
            task()

    @pl.when(s == last)
    def _():
        mix_old[...] = mix_new[...]
        for task in post_tasks():
            task()


def _mixpost(kind, z, mixer_consts, x, mod, g, wo, w1, w3, w2, final_g, final):
    b, length, d = x.shape
    assert length % MIX_TILE == 0
    n_tiles = length // MIX_TILE
    total = b * n_tiles

    def mix_tile(s):
        t = jnp.minimum(s, total - 1)
        return t // n_tiles, t % n_tiles

    def post_tile(s):
        t = jnp.maximum(s - 1, 0)
        return t // n_tiles, t % n_tiles

    def tile_spec(width, col_block):
        return pl.BlockSpec((1, MIX_TILE, width), lambda s: (*mix_tile(s), col_block))

    def halo_specs(rows, col_block):
        per_tile = MIX_TILE // rows
        n_halo = length // rows

        def before(s):
            i, j = mix_tile(s)
            return i, jnp.maximum(j * per_tile - 1, 0), col_block

        def after(s):
            i, j = mix_tile(s)
            return i, jnp.minimum((j + 1) * per_tile, n_halo - 1), col_block

        return (pl.BlockSpec((1, rows, 512), before), pl.BlockSpec((1, rows, 512), after))

    if kind == "even":
        rows = length // GRID_W
        assert rows >= NA_KEY_ROWS
        bias, pool_w, pool_scale = mixer_consts

        def win_map(s, col0):
            i, j = mix_tile(s)
            win0 = jnp.clip(j * ROW_TILE - NA_WIN_ROWS // 2, 0, rows - NA_KEY_ROWS)
            return i, pl.multiple_of(win0 * GRID_W, GRID_W), col0

        win = (pl.Element(1), pl.Element(NA_KEY_ROWS * GRID_W), pl.Element(NA_WIDTH))
        u_before, u_after = halo_specs(POOL_EDGE, 3)
        mix_specs = [
            tile_spec(NA_WIDTH, 0),
            pl.BlockSpec(win, lambda s: win_map(s, NA_WIDTH)),
            pl.BlockSpec(win, lambda s: win_map(s, 2 * NA_WIDTH)),
            _const_spec(bias.shape),
            u_before, tile_spec(512, 3), u_after,
            _const_spec(pool_w.shape),
            _const_spec((1, 512)),
        ]
        mix_args = [z, z, z, bias, z, z, z, pool_w, pool_scale.reshape(1, 512)]
        mixer, mixer_kw = _even_tasks, dict(rows=rows, length=length)
        ext = pltpu.VMEM((MIX_TILE + 2 * POOL_PAD, 512), BF16)
    else:
        assert length >= SWA_WIN
        sink_rows, conv_w, conv_b = mixer_consts

        def win_map(s):
            i, j = mix_tile(s)
            start = jnp.clip(j * MIX_TILE - SWA_BLOCK, 0, length - SWA_WIN)
            return i, pl.multiple_of(start, SWA_BLOCK), ODD_OUT - 2 * SWA_KV_WIDTH

        u_before, u_after = halo_specs(CONV_HALO, 2)
        mix_specs = [
            tile_spec(512, 0),
            pl.BlockSpec((pl.Element(1), pl.Element(SWA_WIN), pl.Element(2 * SWA_KV_WIDTH)), win_map),
            _const_spec((2 * SWA_GROUP * SWA_BLOCK, 1)),
            _const_spec((3, SWA_BLOCK, SWA_KEYS)),
            tile_spec(512, 1),
            u_before, tile_spec(512, 2), u_after,
            _const_spec((3, CONV_WIDTH)),
            _const_spec((1, CONV_WIDTH)),
        ]
        mix_args = [z, z, sink_rows, _swa_masks(), z, z, z, z, conv_w, conv_b.reshape(1, CONV_WIDTH)]
        mixer, mixer_kw = _odd_tasks, dict(length=length)
        ext = pltpu.VMEM((MIX_TILE + 2 * CONV_HALO, 512), F32)

    post_specs = [
        pl.BlockSpec((1, MIX_TILE, d), lambda s: (*post_tile(s), 0)),
        pl.BlockSpec((1, 6, d), lambda s: (post_tile(s)[0], 0, 0)),
        _const_spec((1, d)),
        _const_spec(wo.shape),
        _const_spec(w1.shape),
        _const_spec(w3.shape),
        _const_spec(w2.shape),
        _const_spec((1, d)),
    ]
    post_args = [x, mod, g.reshape(1, d), wo, w1, w3, w2, final_g.reshape(1, d)]
    return pl.pallas_call(
        functools.partial(_mixpost_kernel, n_mix=len(mix_specs), mixer=mixer, final=final, n_tiles=n_tiles,
                          mixer_kw=mixer_kw),
        grid=(total + 1,),
        in_specs=mix_specs + post_specs,
        out_specs=pl.BlockSpec((1, MIX_TILE, d), lambda s: (*post_tile(s), 0)),
        out_shape=jax.ShapeDtypeStruct((b, length, d), F32),
        scratch_shapes=[pltpu.VMEM((MIX_TILE, D_MODEL), BF16), pltpu.VMEM((MIX_TILE, D_MODEL), BF16),
                        pltpu.VMEM((MIX_TILE, D_FF), BF16), ext],
        compiler_params=_cparams(1),
        name="mixpost_" + kind,
    )(*mix_args, *post_args)


def _permute_odd_in(w):
    d = w.shape[0]
    half = HEAD_DIM // 2
    q = w[:, :SWA_WIDTH].reshape(d, 2, SWA_GROUP, 2, half)
    q = jnp.transpose(q, (0, 2, 3, 1, 4)).reshape(d, SWA_WIDTH)
    k = w[:, SWA_WIDTH:SWA_WIDTH + SWA_KV_WIDTH].reshape(d, 2, 2, half)
    k = jnp.transpose(k, (0, 2, 1, 3)).reshape(d, SWA_KV_WIDTH)
    return jnp.concatenate([q, k, w[:, SWA_WIDTH + SWA_KV_WIDTH:]], axis=1)


def _permute_odd_out(w):
    d = w.shape[1]
    a = w[:SWA_WIDTH].reshape(2, SWA_GROUP, HEAD_DIM, d)
    a = jnp.transpose(a, (1, 0, 2, 3)).reshape(SWA_WIDTH, d)
    return jnp.concatenate([a, w[SWA_WIDTH:]], axis=0)


def _fold_qk_scale(w_in, q_width):
    scale = HEAD_DIM ** -0.5
    assert np.log2(scale) == round(np.log2(scale))
    col = jnp.arange(w_in.shape[-1]) < q_width
    return w_in * jnp.where(col, scale, 1.0).astype(w_in.dtype)


def _sink_rows(sink):
    return jnp.repeat(sink.astype(F32), SWA_BLOCK).reshape(2 * SWA_GROUP * SWA_BLOCK, 1)


def _trunk(x, mods, p):
    depth = len(p["ffn_w1"])
    for i in range(depth):
        jdx = i // 2
        mod = mods[i]
        if i % 2 == 0:
            kind = "even"
            z = _pre(x, mod, p["norm_g"][i, 0], p["even_w_in"][jdx])
            consts = (p["na_bias"][jdx], p["pool_w"][jdx], p["pool_scale"][jdx])
            wo = p["even_w_out"][jdx]
        else:
            kind = "odd"
            z = _pre(x, mod, p["norm_g"][i, 0], p["odd_w_in"][jdx], rope=p["rope"])
            consts = (p["sink_rows"][jdx], p["conv_w"][jdx], p["conv_b"][jdx])
            wo = p["odd_w_out"][jdx]
        x = _mixpost(kind, z, consts, x, mod, p["norm_g"][i, 1], wo, p["ffn_w1"][i], p["ffn_w3"][i],
                     p["ffn_w2"][i], p["final_g"], final=(i == depth - 1))
    return x


def kernel(x_prompt, x_sample, c_prompt, c_sample, ada_w, ada_b, norm_g, final_g, ffn_w1, ffn_w3, ffn_w2,
           even_w_in, na_rpb, pool_w, pool_scale, even_w_out, odd_w_in, swa_sink, conv_w, conv_b, odd_w_out):
    depth = ada_w.shape[0]
    nb_p, nb_s = c_prompt.shape[0], c_sample.shape[0]
    pad = (-(nb_p + nb_s)) % 8
    c_all = jnp.concatenate([c_prompt, c_sample, jnp.zeros((pad, D_MODEL), F32)], axis=0)
    mod = _ada_mod(c_all, ada_w, ada_b)
    mods_p = [mod[i, :nb_p].reshape(nb_p, 6, D_MODEL) for i in range(depth)]
    mods_s = [mod[i, nb_p:nb_p + nb_s].reshape(nb_s, 6, D_MODEL) for i in range(depth)]

    max_len = max(x_prompt.shape[1], x_sample.shape[1])
    n_even, n_odd = even_w_in.shape[0], odd_w_in.shape[0]
    p = {
        "norm_g": norm_g, "final_g": final_g,
        "ffn_w1": [ffn_w1[i].astype(BF16) for i in range(depth)],
        "ffn_w3": [ffn_w3[i].astype(BF16) for i in range(depth)],
        "ffn_w2": [ffn_w2[i].astype(BF16) for i in range(depth)],
        "even_w_in": [_fold_qk_scale(even_w_in[j], NA_WIDTH).astype(BF16) for j in range(n_even)],
        "even_w_out": [even_w_out[j].astype(BF16) for j in range(n_even)],
        "na_bias": [_bias_slabs(na_rpb[j]) for j in range(n_even)],
        "pool_w": [pool_w[j].astype(BF16) for j in range(n_even)],
        "pool_scale": pool_scale,
        "odd_w_in": [_permute_odd_in(_fold_qk_scale(odd_w_in[j], SWA_WIDTH)).astype(BF16) for j in range(n_odd)],
        "odd_w_out": [_permute_odd_out(odd_w_out[j]).astype(BF16) for j in range(n_odd)],
        "sink_rows": [_sink_rows(swa_sink[j]) for j in range(n_odd)],
        "conv_w": conv_w, "conv_b": conv_b,
        "rope": _rope_tables(max_len),
    }
    return (_trunk(x_prompt, mods_p, p), _trunk(x_sample, mods_s, p))
```

```python
import functools

import numpy as np
import jax
import jax.numpy as jnp
from jax import lax
from jax.experimental import pallas as pl
from jax.experimental.pallas import tpu as pltpu

F32 = jnp.float32
BF16 = jnp.bfloat16

D_MODEL = 1024
HEAD_DIM = 64
LANES = 128
GRID_W = 64
NA_WIN_ROWS = 8
NA_WIN_COLS = 16
NA_WIDTH = 512
NA_PAIRS = NA_WIDTH // LANES
POOL_WINDOWS = (2, 4, 8, 16)
POOL_GROUP_DIM = 128
POOL_BLK = 128
POOL_PAD = 64
POOL_EDGE = 16
CONV_HALO = 8
EVEN_IN = 2048
SWA_WIDTH = 512
SWA_KV_WIDTH = 128
SWA_WINDOW = 128
SWA_BLOCK = 128
CONV_WIDTH = 512
ODD_IN = 2304
ODD_OUT = 1792
D_FF = 2816
ROPE_THETA = 10000.0
EPS = 1e-6
NEG_INF = float("-inf")

ROW_TILE = 8
MIX_TILE = ROW_TILE * GRID_W
NA_KEY_ROWS = ROW_TILE + NA_WIN_ROWS - 1
NA_LOOKAHEAD = 3
SWA_LOOKAHEAD = 2
POOL_LOOKAHEAD = 2
PRE_TILE = 1024
VMEM_LIMIT = 56 * 1024 * 1024


def _cparams(n_axes):
    return pltpu.CompilerParams(
        dimension_semantics=("arbitrary",) * n_axes,
        vmem_limit_bytes=VMEM_LIMIT,
    )


def _const_spec(shape):
    nd = len(shape)
    return pl.BlockSpec(shape, lambda *_: (0,) * nd, pipeline_mode=pl.Buffered(1))


ADA_TILE = 1536


def _ada_kernel(c_ref, w_ref, b_ref, o_ref):
    c = c_ref[...]
    act = c * jax.nn.sigmoid(c)
    o_ref[0] = jnp.dot(act, w_ref[0], precision=lax.Precision.HIGHEST,
                       preferred_element_type=F32) + b_ref[0]


def _ada_mod(c_all, ada_w, ada_b):
    depth, d, n = ada_w.shape
    rows = c_all.shape[0]
    return pl.pallas_call(
        _ada_kernel,
        grid=(depth, n // ADA_TILE),
        in_specs=[
            pl.BlockSpec((rows, d), lambda l, j: (0, 0)),
            pl.BlockSpec((1, d, ADA_TILE), lambda l, j: (l, 0, j)),
            pl.BlockSpec((1, 1, ADA_TILE), lambda l, j: (l, 0, j)),
        ],
        out_specs=pl.BlockSpec((1, rows, ADA_TILE), lambda l, j: (l, 0, j)),
        out_shape=jax.ShapeDtypeStruct((depth, rows, n), F32),
        compiler_params=_cparams(2),
        name="ada_mod",
    )(c_all, ada_w, ada_b.reshape(depth, 1, n))


ROPE_TILE = 1024
ROPE_FREQS = HEAD_DIM // 2
ROPE_PACK = LANES // ROPE_FREQS


def _rope_kernel(inv_ref, cos_ref, sin_ref):
    tile = cos_ref.shape[0]
    i = pl.program_id(0)
    row = lax.broadcasted_iota(jnp.int32, (tile, LANES), 0) + i * tile
    lane = lax.broadcasted_iota(jnp.int32, (tile, LANES), 1)
    pos = (row * ROPE_PACK + lane // ROPE_FREQS).astype(F32)
    ang = pos * inv_ref[...]
    cos_ref[...] = jnp.cos(ang)
    sin_ref[...] = jnp.sin(ang)


def _rope_tables(length):
    inv = 1.0 / (ROPE_THETA ** (jnp.arange(0, HEAD_DIM, 2, dtype=F32) / HEAD_DIM))
    inv_lanes = jnp.tile(inv, ROPE_PACK).reshape(1, LANES)
    rows = length // ROPE_PACK
    tile = min(ROPE_TILE, rows)
    packed = pl.pallas_call(
        _rope_kernel,
        grid=(rows // tile,),
        in_specs=[pl.BlockSpec((1, LANES), lambda i: (0, 0))],
        out_specs=[pl.BlockSpec((tile, LANES), lambda i: (i, 0))] * 2,
        out_shape=[jax.ShapeDtypeStruct((rows, LANES), F32)] * 2,
        compiler_params=_cparams(1),
        name="rope_tables",
    )(inv_lanes)
    return [jnp.tile(t.reshape(length, ROPE_FREQS), (1, ROPE_PACK)) for t in packed]


RPB_ROWS = 2 * NA_WIN_ROWS - 1
RPB_COLS = 2 * NA_WIN_COLS - 1
NA_KEYS = NA_WIN_ROWS * GRID_W


def _bias_kernel(rep_ref, o_ref, w_scr):
    shape = (GRID_W, RPB_ROWS * GRID_W)
    qc = lax.broadcasted_iota(jnp.int32, shape, 0)
    kc = lax.broadcasted_iota(jnp.int32, shape, 1) % GRID_W
    dc = kc - qc + (NA_WIN_COLS - 1)
    cs = jnp.clip(qc - NA_WIN_COLS // 2, 0, GRID_W - NA_WIN_COLS)
    ok = (kc >= cs) & (kc < cs + NA_WIN_COLS)
    for hh in range(2):
        acc = jnp.zeros(shape, F32)
        for d in range(RPB_COLS):
            acc = jnp.where(dc == d, rep_ref[hh, d:d + 1, :], acc)
        w_scr[...] = jnp.where(ok, acc, NEG_INF)
        for d0 in range(NA_WIN_ROWS):
            o_ref[d0, 0, hh * GRID_W:(hh + 1) * GRID_W, :] = w_scr[:, d0 * GRID_W:d0 * GRID_W + NA_KEYS]


def _bias_slabs(rpb):
    heads = rpb.shape[0]
    rep = jnp.repeat(jnp.transpose(rpb, (0, 2, 1)), GRID_W, axis=-1)
    return pl.pallas_call(
        _bias_kernel,
        grid=(heads // 2,),
        in_specs=[pl.BlockSpec((2, RPB_COLS, RPB_ROWS * GRID_W), lambda p: (p, 0, 0))],
        out_specs=pl.BlockSpec((NA_WIN_ROWS, 1, 2 * GRID_W, NA_KEYS), lambda p: (0, p, 0, 0)),
        out_shape=jax.ShapeDtypeStruct((NA_WIN_ROWS, heads // 2, 2 * GRID_W, NA_KEYS), F32),
        scratch_shapes=[pltpu.VMEM((GRID_W, RPB_ROWS * GRID_W), F32)],
        compiler_params=_cparams(1),
        name="na_bias_slabs",
    )(rep)


def _norm_modulate(x, g, shift, scale):
    y = x * lax.rsqrt(jnp.mean(x * x, axis=-1, keepdims=True) + EPS) * g
    return y * (1.0 + scale) + shift


def _pre_even_kernel(x_ref, mod_ref, g_ref, w_ref, o_ref):
    h = _norm_modulate(x_ref[0], g_ref[...], mod_ref[0, 0:1, :], mod_ref[0, 1:2, :])
    z = jnp.dot(h.astype(BF16), w_ref[...], preferred_element_type=F32)
    o_ref[0] = z.astype(BF16)


def _pre_odd_kernel(x_ref, mod_ref, g_ref, w_ref, cos_ref, sin_ref, o_ref):
    h = _norm_modulate(x_ref[0], g_ref[...], mod_ref[0, 0:1, :], mod_ref[0, 1:2, :])
    z = jnp.dot(h.astype(BF16), w_ref[...], preferred_element_type=F32)
    c = cos_ref[...]
    lane = lax.broadcasted_iota(jnp.int32, c.shape, 1)
    s = jnp.where(lane < LANES // 2, -sin_ref[...], sin_ref[...])

    def rope(t):
        return t * c + pltpu.roll(t, LANES // 2, 1) * s

    for blk in range(SWA_WIDTH // LANES):
        q = rope(z[:, blk * LANES:(blk + 1) * LANES])
        o_ref[0, :, blk * LANES:(blk + 1) * LANES] = q.astype(BF16)
    k0 = SWA_WIDTH
    v0 = k0 + SWA_KV_WIDTH
    b0 = v0 + SWA_KV_WIDTH
    c0 = b0 + CONV_WIDTH
    x0 = c0 + CONV_WIDTH
    o_ref[0, :, 512:1024] = z[:, b0:c0].astype(BF16)
    o_ref[0, :, 1024:1536] = (z[:, c0:x0] * z[:, x0:x0 + CONV_WIDTH]).astype(BF16)
    o_ref[0, :, 1536:1664] = rope(z[:, k0:v0]).astype(BF16)
    o_ref[0, :, 1664:1792] = z[:, v0:b0].astype(BF16)


def _pre(x, mod, g, w, rope=None):
    b, length, d = x.shape
    n_in = w.shape[1]
    tile = min(PRE_TILE, length)
    in_specs = [
        pl.BlockSpec((1, tile, d), lambda i, j: (i, j, 0)),
        pl.BlockSpec((1, 6, d), lambda i, j: (i, 0, 0)),
        _const_spec((1, d)),
        _const_spec((d, n_in)),
    ]
    args = [x, mod, g.reshape(1, d), w]
    if rope is None:
        kern, n_out = _pre_even_kernel, n_in
    else:
        kern, n_out = _pre_odd_kernel, ODD_OUT
        in_specs += [pl.BlockSpec((tile, LANES), lambda i, j: (j, 0))] * 2
        args += list(rope)
    return pl.pallas_call(
        kern,
        grid=(b, length // tile),
        in_specs=in_specs,
        out_specs=pl.BlockSpec((1, tile, n_out), lambda i, j: (i, j, 0)),
        out_shape=jax.ShapeDtypeStruct((b, length, n_out), BF16),
        compiler_params=_cparams(2),
        name="pre_even" if rope is None else "pre_odd",
    )(*args)


def _interleave(main, side):
    out, done = [], 0
    for i, task in enumerate(main):
        out.append(task)
        want = (i + 1) * len(side) // len(main)
        out.extend(side[done:want])
        done = want
    return out


def _pipelined(produce, consume, items, lookahead):
    pending = {}

    def step(idx):
        if idx < len(items):
            pending[idx] = produce(*items[idx])
        if idx >= lookahead:
            done = idx - lookahead
            consume(*items[done], pending.pop(done))

    return [functools.partial(step, idx) for idx in range(len(items) + lookahead)]


def _even_tasks(q_ref, k_ref, v_ref, bias_ref, up_ref, uc_ref, un_ref, pw_ref, ps_ref, out_ref, ext_ref,
                *, j, n_tiles, rows, length):
    row0 = j * ROW_TILE
    win0 = jnp.clip(row0 - NA_WIN_ROWS // 2, 0, rows - NA_KEY_ROWS)
    lane = lax.broadcasted_iota(jnp.int32, (GRID_W, LANES), 1)
    low = lane < HEAD_DIM
    ones_keys = jnp.ones((NA_KEYS, LANES), BF16)

    koffs, d0s = [], []
    for r in range(ROW_TILE):
        row = row0 + r
        rs = jnp.clip(row - NA_WIN_ROWS // 2, 0, rows - NA_WIN_ROWS)
        koffs.append(pl.multiple_of((rs - win0) * GRID_W, GRID_W))
        d0s.append(rs - row + (NA_WIN_ROWS - 1))

    def scores(r, p):
        cols = slice(p * LANES, (p + 1) * LANES)
        qp = q_ref[0, r * GRID_W:(r + 1) * GRID_W, cols]
        zero = jnp.zeros_like(qp)
        qbd = jnp.concatenate([jnp.where(low, qp, zero), jnp.where(low, zero, qp)], axis=0)
        kp = k_ref[0, pl.ds(koffs[r], NA_KEYS), cols]
        s = lax.dot_general(qbd, kp, (((1,), (1,)), ((), ())), preferred_element_type=F32)
        return s + bias_ref[d0s[r], p]

    def attend(r, p, s):
        cols = slice(p * LANES, (p + 1) * LANES)
        m = jnp.max(s, axis=-1, keepdims=True)
        e = jnp.exp(s - m).astype(BF16)
        vp = jnp.concatenate([v_ref[0, pl.ds(koffs[r], NA_KEYS), cols], ones_keys], axis=1)
        oa = jnp.dot(e, vp, preferred_element_type=F32)
        o = oa[:, :LANES] * (1.0 / oa[:, LANES:])
        out_ref[r * GRID_W:(r + 1) * GRID_W, cols] = jnp.where(low, o[:GRID_W], o[GRID_W:]).astype(BF16)

    first = j == 0
    last = j == n_tiles - 1
    zpad = jnp.zeros((POOL_PAD - POOL_EDGE, 512), BF16)
    prev = up_ref[0]
    nxt = un_ref[0]
    ext_ref[0:POOL_PAD - POOL_EDGE, :] = zpad
    ext_ref[POOL_PAD - POOL_EDGE:POOL_PAD, :] = jnp.where(first, jnp.zeros_like(prev), prev)
    ext_ref[POOL_PAD:POOL_PAD + MIX_TILE, :] = uc_ref[0]
    ext_ref[POOL_PAD + MIX_TILE:POOL_PAD + MIX_TILE + POOL_EDGE, :] = jnp.where(last, jnp.zeros_like(nxt), nxt)
    ext_ref[POOL_PAD + MIX_TILE + POOL_EDGE:, :] = zpad
    band_shape = (POOL_BLK, POOL_BLK + 2 * POOL_PAD)
    delta = (lax.broadcasted_iota(jnp.int32, band_shape, 1)
             - lax.broadcasted_iota(jnp.int32, band_shape, 0) - POOL_PAD)
    bands = [jnp.where((delta >= -(w // 2)) & (delta < w - w // 2), 1.0, 0.0).astype(BF16)
             for w in POOL_WINDOWS]

    def pool_sum(half, g):
        cols = slice(g * POOL_GROUP_DIM, (g + 1) * POOL_GROUP_DIM)
        xs = [ext_ref[blk * POOL_BLK:blk * POOL_BLK + POOL_BLK + 2 * POOL_PAD, cols]
              for blk in (2 * half, 2 * half + 1)]
        return jnp.dot(bands[g], jnp.concatenate(xs, axis=1), preferred_element_type=F32)

    def pool_project(half, g, acc2):
        w = POOL_WINDOWS[g]
        cols = slice(g * POOL_GROUP_DIM, (g + 1) * POOL_GROUP_DIM)
        mixed = []
        for k, blk in enumerate((2 * half, 2 * half + 1)):
            base = blk * POOL_BLK
            u = ext_ref[base + POOL_PAD:base + POOL_PAD + POOL_BLK, cols].astype(F32)
            t = lax.broadcasted_iota(jnp.int32, (POOL_BLK, 1), 0) + (j * MIX_TILE + base)
            lo = jnp.clip(t - w // 2, 0, length)
            hi = jnp.clip(t - w // 2 + w, 0, length)
            acc = acc2[:, k * POOL_GROUP_DIM:(k + 1) * POOL_GROUP_DIM]
            mixed.append((acc / (hi - lo).astype(F32) - u).astype(BF16))
        y = jnp.dot(jnp.concatenate(mixed, axis=0), pw_ref[g], preferred_element_type=F32) * ps_ref[:, cols]
        base = 2 * half * POOL_BLK
        out_ref[base:base + 2 * POOL_BLK, NA_WIDTH + g * POOL_GROUP_DIM:NA_WIDTH + (g + 1) * POOL_GROUP_DIM] = (
            y.astype(BF16))

    units = [(r, p) for r in range(ROW_TILE) for p in range(NA_PAIRS)]
    pools = [(half, g) for half in range(MIX_TILE // (2 * POOL_BLK)) for g in range(len(POOL_WINDOWS))]
    return _interleave(_pipelined(scores, attend, units, NA_LOOKAHEAD),
                       _pipelined(pool_sum, pool_project, pools, POOL_LOOKAHEAD))


SWA_KEYS = 3 * SWA_BLOCK
SWA_WIN = MIX_TILE + 2 * SWA_BLOCK
SWA_GROUP = 4
Q_BLOCKS = MIX_TILE // SWA_BLOCK


def _odd_tasks(q_ref, kv_ref, sink_ref, mask_ref, bg_ref, up_ref, uc_ref, un_ref, cw_ref, cb_ref,
               out_ref, ext_ref, *, j, n_tiles, length):
    t0 = j * MIX_TILE
    win0 = jnp.clip(t0 - SWA_BLOCK, 0, length - SWA_WIN)
    lane = lax.broadcasted_iota(jnp.int32, (SWA_BLOCK, LANES), 1)
    qk_a = (lane // (HEAD_DIM // 2)) % 2 == 0
    v_a = lane < HEAD_DIM
    stack = SWA_GROUP * SWA_BLOCK
    ones_keys = jnp.ones((SWA_KEYS, LANES), BF16)
    koffs, rels = [], []
    for qb in range(Q_BLOCKS):
        qs = t0 + qb * SWA_BLOCK
        ks = jnp.clip(qs - SWA_BLOCK, 0, length - SWA_KEYS)
        koffs.append(pl.multiple_of(ks - win0, SWA_BLOCK))
        rels.append((qs - ks) // SWA_BLOCK)

    def scores(qb, c):
        rows_q = slice(qb * SWA_BLOCK, (qb + 1) * SWA_BLOCK)
        qc = q_ref[0, rows_q, c * LANES:(c + 1) * LANES]
        zero = jnp.zeros_like(qc)
        qst = jnp.concatenate([jnp.where(qk_a, qc, zero), jnp.where(qk_a, zero, qc)], axis=0)
        kp = kv_ref[0, pl.ds(koffs[qb], SWA_KEYS), 0:LANES]
        return lax.dot_general(qst, kp, (((1,), (1,)), ((), ())), preferred_element_type=F32)

    def attend(qb, c, s):
        rows_q = slice(qb * SWA_BLOCK, (qb + 1) * SWA_BLOCK)
        mask = mask_ref[rels[qb]]
        s = jnp.concatenate([s[:SWA_BLOCK] + mask, s[SWA_BLOCK:] + mask], axis=0)
        sink = jnp.concatenate([sink_ref[c * SWA_BLOCK:(c + 1) * SWA_BLOCK],
                                sink_ref[stack + c * SWA_BLOCK:stack + (c + 1) * SWA_BLOCK]], axis=0)
        m = jnp.maximum(jnp.max(s, axis=-1, keepdims=True), sink)
        e = jnp.exp(s - m).astype(BF16)
        vp = jnp.concatenate([kv_ref[0, pl.ds(koffs[qb], SWA_KEYS), LANES:2 * LANES], ones_keys], axis=1)
        oa = jnp.dot(e, vp, preferred_element_type=F32)
        o = oa[:, :LANES] * (1.0 / (oa[:, LANES:] + jnp.exp(sink - m)))
        out_ref[rows_q, c * LANES:(c + 1) * LANES] = jnp.where(v_a, o[:SWA_BLOCK], o[SWA_BLOCK:]).astype(BF16)

    first = j == 0
    last = j == n_tiles - 1
    ext_ref[0:CONV_HALO, :] = jnp.where(first, 0.0, up_ref[0].astype(F32))
    ext_ref[CONV_HALO:CONV_HALO + MIX_TILE, :] = uc_ref[0].astype(F32)
    ext_ref[CONV_HALO + MIX_TILE:, :] = jnp.where(last, 0.0, un_ref[0].astype(F32))

    def conv_chunk(blk):
        rows_c = slice(blk * SWA_BLOCK, (blk + 1) * SWA_BLOCK)
        conv = cb_ref[...]
        for jj in range(3):
            off = CONV_HALO - 1 + jj + blk * SWA_BLOCK
            conv = conv + ext_ref[off:off + SWA_BLOCK, :] * cw_ref[jj:jj + 1, :]
        out_ref[rows_c, SWA_WIDTH:SWA_WIDTH + CONV_WIDTH] = (
            bg_ref[0, rows_c, :].astype(F32) * conv).astype(BF16)

    units = [(qb, c) for qb in range(Q_BLOCKS) for c in range(SWA_GROUP)]
    return _interleave(_pipelined(scores, attend, units, SWA_LOOKAHEAD),
                       [functools.partial(conv_chunk, blk) for blk in range(Q_BLOCKS)])


def _swa_masks():
    q = np.arange(SWA_BLOCK)[:, None]
    k = np.arange(SWA_KEYS)[None, :]
    slabs = [np.where(np.abs(q + rel * SWA_BLOCK - k) <= SWA_WINDOW, 0.0, -np.inf) for rel in range(3)]
    return jnp.asarray(np.stack(slabs), F32)


POST_CHUNK = 256


def _post_tasks(x_ref, mix_ref, mod_ref, g_ref, wo_ref, w1_ref, w3_ref, w2_ref, fg_ref, o_ref, hid_ref,
                *, final):
    st = {"x1": [], "x2": []}
    n_d = D_MODEL // POST_CHUNK
    n_ff = D_FF // POST_CHUNK

    def proj_out(c):
        cols = slice(c * POST_CHUNK, (c + 1) * POST_CHUNK)
        mix = jnp.dot(mix_ref[...], wo_ref[:, cols], preferred_element_type=F32)
        st["x1"].append(x_ref[0, :, cols] + mod_ref[0, 2:3, cols] * mix)

    def norm():
        x1 = jnp.concatenate(st["x1"], axis=1)
        st["x1_full"] = x1
        st["h"] = _norm_modulate(x1, g_ref[...], mod_ref[0, 3:4, :], mod_ref[0, 4:5, :]).astype(BF16)

    def up(c):
        cols = slice(c * POST_CHUNK, (c + 1) * POST_CHUNK)
        a = jnp.dot(st["h"], w1_ref[:, cols], preferred_element_type=F32)
        b = jnp.dot(st["h"], w3_ref[:, cols], preferred_element_type=F32)
        hid_ref[:, cols] = (a * jax.nn.sigmoid(a) * b).astype(BF16)

    def down(c):
        cols = slice(c * POST_CHUNK, (c + 1) * POST_CHUNK)
        f = jnp.dot(hid_ref[...], w2_ref[:, cols], preferred_element_type=F32)
        x2 = st["x1_full"][:, cols] + mod_ref[0, 5:6, cols] * f
        if final:
            st["x2"].append(x2)
        else:
            o_ref[0, :, cols] = x2

    def final_norm():
        x2 = jnp.concatenate(st["x2"], axis=1)
        o_ref[0] = x2 * lax.rsqrt(jnp.mean(x2 * x2, axis=-1, keepdims=True) + EPS) * fg_ref[...]

    tasks = [functools.partial(proj_out, c) for c in range(n_d)] + [norm]
    tasks += [functools.partial(up, c) for c in range(n_ff)]
    tasks += [functools.partial(down, c) for c in range(n_d)]
    if final:
        tasks.append(final_norm)
    return tasks


def _mixpost_kernel(*refs, n_mix, mixer, final, n_tiles, mixer_kw):
    mix_refs = refs[:n_mix]
    (x_ref, mod_ref, g_ref, wo_ref, w1_ref, w3_ref, w2_ref, fg_ref, o_ref,
     mix_new, mix_old, hid_ref, ext_ref) = refs[n_mix:]
    s = pl.program_id(0)
    last = pl.num_programs(0) - 1

    def mixer_tasks():
        return mixer(*mix_refs, mix_new, ext_ref, j=s % n_tiles, n_tiles=n_tiles, **mixer_kw)

    def post_tasks():
        return _post_tasks(x_ref, mix_old, mod_ref, g_ref, wo_ref, w1_ref, w3_ref, w2_ref, fg_ref, o_ref,
                           hid_ref, final=final)

    @pl.when(s == 0)
    def _():
        for task in mixer_tasks():
            task()

    @pl.when(jnp.logical_and(s > 0, s < last))
    def _():
        mix_old[...] = mix_new[...]
        for task in _interleave(post_tasks(), mixer_tasks()):
            task()

    @pl.when(s == last)
    def _():
        mix_old[...] = mix_new[...]
        for task in post_tasks():
            task()


def _mixpost(kind, z, mixer_consts, x, mod, g, wo, w1, w3, w2, final_g, final):
    b, length, d = x.shape
    assert length % MIX_TILE == 0
    n_tiles = length // MIX_TILE
    total = b * n_tiles

    def mix_tile(s):
        t = jnp.minimum(s, total - 1)
        return t // n_tiles, t % n_tiles

    def post_tile(s):
        t = jnp.maximum(s - 1, 0)
        return t // n_tiles, t % n_tiles

    def tile_spec(width, col_block):
        return pl.BlockSpec((1, MIX_TILE, width), lambda s: (*mix_tile(s), col_block))

    def halo_specs(rows, col_block):
        per_tile = MIX_TILE // rows
        n_halo = length // rows

        def before(s):
            i, j = mix_tile(s)
            return i, jnp.maximum(j * per_tile - 1, 0), col_block

        def after(s):
            i, j = mix_tile(s)
            return i, jnp.minimum((j + 1) * per_tile, n_halo - 1), col_block

        return (pl.BlockSpec((1, rows, 512), before), pl.BlockSpec((1, rows, 512), after))

    if kind == "even":
        rows = length // GRID_W
        assert rows >= NA_KEY_ROWS
        bias, pool_w, pool_scale = mixer_consts

        def win_map(s, col0):
            i, j = mix_tile(s)
            win0 = jnp.clip(j * ROW_TILE - NA_WIN_ROWS // 2, 0, rows - NA_KEY_ROWS)
            return i, pl.multiple_of(win0 * GRID_W, GRID_W), col0

        win = (pl.Element(1), pl.Element(NA_KEY_ROWS * GRID_W), pl.Element(NA_WIDTH))
        u_before, u_after = halo_specs(POOL_EDGE, 3)
        mix_specs = [
            tile_spec(NA_WIDTH, 0),
            pl.BlockSpec(win, lambda s: win_map(s, NA_WIDTH)),
            pl.BlockSpec(win, lambda s: win_map(s, 2 * NA_WIDTH)),
            _const_spec(bias.shape),
            u_before, tile_spec(512, 3), u_after,
            _const_spec(pool_w.shape),
            _const_spec((1, 512)),
        ]
        mix_args = [z, z, z, bias, z, z, z, pool_w, pool_scale.reshape(1, 512)]
        mixer, mixer_kw = _even_tasks, dict(rows=rows, length=length)
        ext = pltpu.VMEM((MIX_TILE + 2 * POOL_PAD, 512), BF16)
    else:
        assert length >= SWA_WIN
        sink_rows, conv_w, conv_b = mixer_consts

        def win_map(s):
            i, j = mix_tile(s)
            start = jnp.clip(j * MIX_TILE - SWA_BLOCK, 0, length - SWA_WIN)
            return i, pl.multiple_of(start, SWA_BLOCK), ODD_OUT - 2 * SWA_KV_WIDTH

        u_before, u_after = halo_specs(CONV_HALO, 2)
        mix_specs = [
            tile_spec(512, 0),
            pl.BlockSpec((pl.Element(1), pl.Element(SWA_WIN), pl.Element(2 * SWA_KV_WIDTH)), win_map),
            _const_spec((2 * SWA_GROUP * SWA_BLOCK, 1)),
            _const_spec((3, SWA_BLOCK, SWA_KEYS)),
            tile_spec(512, 1),
            u_before, tile_spec(512, 2), u_after,
            _const_spec((3, CONV_WIDTH)),
            _const_spec((1, CONV_WIDTH)),
        ]
        mix_args = [z, z, sink_rows, _swa_masks(), z, z, z, z, conv_w, conv_b.reshape(1, CONV_WIDTH)]
        mixer, mixer_kw = _odd_tasks, dict(length=length)
        ext = pltpu.VMEM((MIX_TILE + 2 * CONV_HALO, 512), F32)

    post_specs = [
        pl.BlockSpec((1, MIX_TILE, d), lambda s: (*post_tile(s), 0)),
        pl.BlockSpec((1, 6, d), lambda s: (post_tile(s)[0], 0, 0)),
        _const_spec((1, d)),
        _const_spec(wo.shape),
        _const_spec(w1.shape),
        _const_spec(w3.shape),
        _const_spec(w2.shape),
        _const_spec((1, d)),
    ]
    post_args = [x, mod, g.reshape(1, d), wo, w1, w3, w2, final_g.reshape(1, d)]
    return pl.pallas_call(
        functools.partial(_mixpost_kernel, n_mix=len(mix_specs), mixer=mixer, final=final, n_tiles=n_tiles,
                          mixer_kw=mixer_kw),
        grid=(total + 1,),
        in_specs=mix_specs + post_specs,
        out_specs=pl.BlockSpec((1, MIX_TILE, d), lambda s: (*post_tile(s), 0)),
        out_shape=jax.ShapeDtypeStruct((b, length, d), F32),
        scratch_shapes=[pltpu.VMEM((MIX_TILE, D_MODEL), BF16), pltpu.VMEM((MIX_TILE, D_MODEL), BF16),
                        pltpu.VMEM((MIX_TILE, D_FF), BF16), ext],
        compiler_params=_cparams(1),
        name="mixpost_" + kind,
    )(*mix_args, *post_args)


def _permute_odd_in(w):
    d = w.shape[0]
    half = HEAD_DIM // 2
    q = w[:, :SWA_WIDTH].reshape(d, 2, SWA_GROUP, 2, half)
    q = jnp.transpose(q, (0, 2, 3, 1, 4)).reshape(d, SWA_WIDTH)
    k = w[:, SWA_WIDTH:SWA_WIDTH + SWA_KV_WIDTH].reshape(d, 2, 2, half)
    k = jnp.transpose(k, (0, 2, 1, 3)).reshape(d, SWA_KV_WIDTH)
    return jnp.concatenate([q, k, w[:, SWA_WIDTH + SWA_KV_WIDTH:]], axis=1)


def _permute_odd_out(w):
    d = w.shape[1]
    a = w[:SWA_WIDTH].reshape(2, SWA_GROUP, HEAD_DIM, d)
    a = jnp.transpose(a, (1, 0, 2, 3)).reshape(SWA_WIDTH, d)
    return jnp.concatenate([a, w[SWA_WIDTH:]], axis=0)


def _fold_qk_scale(w_in, q_width):
    scale = HEAD_DIM ** -0.5
    assert np.log2(scale) == round(np.log2(scale))
    col = jnp.arange(w_in.shape[-1]) < q_width
    return w_in * jnp.where(col, scale, 1.0).astype(w_in.dtype)


def _sink_rows(sink):
    return jnp.repeat(sink.astype(F32), SWA_BLOCK).reshape(2 * SWA_GROUP * SWA_BLOCK, 1)


def _trunk(x, mods, p):
    depth = len(p["ffn_w1"])
    for i in range(depth):
        jdx = i // 2
        mod = mods[i]
        if i % 2 == 0:
            kind = "even"
            z = _pre(x, mod, p["norm_g"][i, 0], p["even_w_in"][jdx])
            consts = (p["na_bias"][jdx], p["pool_w"][jdx], p["pool_scale"][jdx])
            wo = p["even_w_out"][jdx]
        else:
            kind = "odd"
            z = _pre(x, mod, p["norm_g"][i, 0], p["odd_w_in"][jdx], rope=p["rope"])
            consts = (p["sink_rows"][jdx], p["conv_w"][jdx], p["conv_b"][jdx])
            wo = p["odd_w_out"][jdx]
        x = _mixpost(kind, z, consts, x, mod, p["norm_g"][i, 1], wo, p["ffn_w1"][i], p["ffn_w3"][i],
                     p["ffn_w2"][i], p["final_g"], final=(i == depth - 1))
    return x


def kernel(x_prompt, x_sample, c_prompt, c_sample, ada_w, ada_b, norm_g, final_g, ffn_w1, ffn_w3, ffn_w2,
           even_w_in, na_rpb, pool_w, pool_scale, even_w_out, odd_w_in, swa_sink, conv_w, conv_b, odd_w_out):
    depth = ada_w.shape[0]
    nb_p, nb_s = c_prompt.shape[0], c_sample.shape[0]
    pad = (-(nb_p + nb_s)) % 8
    c_all = jnp.concatenate([c_prompt, c_sample, jnp.zeros((pad, D_MODEL), F32)], axis=0)
    mod = _ada_mod(c_all, ada_w, ada_b)
    mods_p = [mod[i, :nb_p].reshape(nb_p, 6, D_MODEL) for i in range(depth)]
    mods_s = [mod[i, nb_p:nb_p + nb_s].reshape(nb_s, 6, D_MODEL) for i in range(depth)]

    max_len = max(x_prompt.shape[1], x_sample.shape[1])
    n_even, n_odd = even_w_in.shape[0], odd_w_in.shape[0]
    p = {
        "norm_g": norm_g, "final_g": final_g,
        "ffn_w1": [ffn_w1[i].astype(BF16) for i in range(depth)],
        "ffn_w3": [ffn_w3[i].astype(BF16) for i in range(depth)],
        "ffn_w2": [ffn_w2[i].astype(BF16) for i in range(depth)],
        "even_w_in": [_fold_qk_scale(even_w_in[j], NA_WIDTH).astype(BF16) for j in range(n_even)],
        "even_w_out": [even_w_out[j].astype(BF16) for j in range(n_even)],
        "na_bias": [_bias_slabs(na_rpb[j]) for j in range(n_even)],
        "pool_w": [pool_w[j].astype(BF16) for j in range(n_even)],
        "pool_scale": pool_scale,
        "odd_w_in": [_permute_odd_in(_fold_qk_scale(odd_w_in[j], SWA_WIDTH)).astype(BF16) for j in range(n_odd)],
        "odd_w_out": [_permute_odd_out(odd_w_out[j]).astype(BF16) for j in range(n_odd)],
        "sink_rows": [_sink_rows(swa_sink[j]) for j in range(n_odd)],
        "conv_w": conv_w, "conv_b": conv_b,
        "rope": _rope_tables(max_len),
    }
    return (_trunk(x_prompt, mods_p, p), _trunk(x_sample, mods_s, p))
```

```python
import functools

import numpy as np
import jax
import jax.numpy as jnp
from jax import lax
from jax.experimental import pallas as pl
from jax.experimental.pallas import tpu as pltpu

F32 = jnp.float32
BF16 = jnp.bfloat16

D_MODEL = 1024
HEAD_DIM = 64
LANES = 128
GRID_W = 64
NA_WIN_ROWS = 8
NA_WIN_COLS = 16
NA_WIDTH = 512
NA_PAIRS = NA_WIDTH // LANES
POOL_WINDOWS = (2, 4, 8, 16)
POOL_GROUP_DIM = 128
POOL_BLK = 128
POOL_PAD = 64
POOL_EDGE = 16
CONV_HALO = 8
EVEN_IN = 2048
SWA_WIDTH = 512
SWA_KV_WIDTH = 128
SWA_WINDOW = 128
SWA_BLOCK = 128
CONV_WIDTH = 512
ODD_IN = 2304
ODD_OUT = 1792
D_FF = 2816
ROPE_THETA = 10000.0
EPS = 1e-6
NEG_INF = float("-inf")

ROW_TILE = 8
MIX_TILE = ROW_TILE * GRID_W
NA_KEY_ROWS = ROW_TILE + NA_WIN_ROWS - 1
NA_LOOKAHEAD = 3
SWA_LOOKAHEAD = 2
POOL_LOOKAHEAD = 2
PRE_TILE = 1024
VMEM_LIMIT = 56 * 1024 * 1024


def _cparams(n_axes):
    return pltpu.CompilerParams(
        dimension_semantics=("arbitrary",) * n_axes,
        vmem_limit_bytes=VMEM_LIMIT,
    )


def _const_spec(shape):
    nd = len(shape)
    return pl.BlockSpec(shape, lambda *_: (0,) * nd, pipeline_mode=pl.Buffered(1))


def _layer_spec(shape, layer):
    nd = len(shape) - 1
    return pl.BlockSpec((None,) + tuple(shape[1:]), lambda *_: (layer,) + (0,) * nd,
                        pipeline_mode=pl.Buffered(1))


ADA_TILE = 3072


def _ada_kernel(c_ref, w_ref, b_ref, o_ref):
    c = c_ref[...]
    act = c * jax.nn.sigmoid(c)
    o_ref[0] = jnp.dot(act, w_ref[0], precision=lax.Precision.HIGHEST,
                       preferred_element_type=F32) + b_ref[0]


def _ada_mod(c_all, ada_w, ada_b):
    depth, d, n = ada_w.shape
    rows = c_all.shape[0]
    return pl.pallas_call(
        _ada_kernel,
        grid=(depth, n // ADA_TILE),
        in_specs=[
            pl.BlockSpec((rows, d), lambda l, j: (0, 0)),
            pl.BlockSpec((1, d, ADA_TILE), lambda l, j: (l, 0, j)),
            pl.BlockSpec((1, 1, ADA_TILE), lambda l, j: (l, 0, j)),
        ],
        out_specs=pl.BlockSpec((1, rows, ADA_TILE), lambda l, j: (l, 0, j)),
        out_shape=jax.ShapeDtypeStruct((depth, rows, n), F32),
        compiler_params=_cparams(2),
        name="ada_mod",
    )(c_all, ada_w, ada_b.reshape(depth, 1, n))


ROPE_BLOCK = 128


def _rope_kernel(inv_ref, cf_ref, sf_ref, cc_ref, sc_ref):
    inv = inv_ref[...]
    fine = lax.broadcasted_iota(jnp.int32, cf_ref.shape, 0).astype(F32) * inv
    coarse = (lax.broadcasted_iota(jnp.int32, cc_ref.shape, 0) * ROPE_BLOCK).astype(F32) * inv
    cf_ref[...] = jnp.cos(fine)
    sf_ref[...] = jnp.sin(fine)
    cc_ref[...] = jnp.cos(coarse)
    sc_ref[...] = jnp.sin(coarse)


def _rope_tables(length):
    inv = 1.0 / (ROPE_THETA ** (jnp.arange(0, HEAD_DIM, 2, dtype=F32) / HEAD_DIM))
    inv_lanes = jnp.tile(inv, LANES // (HEAD_DIM // 2)).reshape(1, LANES)
    n_coarse = length // ROPE_BLOCK
    shapes = [(ROPE_BLOCK, LANES)] * 2 + [(n_coarse, LANES)] * 2
    return pl.pallas_call(
        _rope_kernel,
        grid=(1,),
        in_specs=[pl.BlockSpec((1, LANES), lambda i: (0, 0))],
        out_specs=[pl.BlockSpec(s, lambda i: (0, 0)) for s in shapes],
        out_shape=[jax.ShapeDtypeStruct(s, F32) for s in shapes],
        compiler_params=_cparams(1),
        name="rope_tables",
    )(inv_lanes)


RPB_ROWS = 2 * NA_WIN_ROWS - 1
RPB_COLS = 2 * NA_WIN_COLS - 1
NA_KEYS = NA_WIN_ROWS * GRID_W


def _bias_kernel(rep_ref, o_ref, w_scr):
    shape = (GRID_W, RPB_ROWS * GRID_W)
    qc = lax.broadcasted_iota(jnp.int32, shape, 0)
    kc = lax.broadcasted_iota(jnp.int32, shape, 1) % GRID_W
    dc = kc - qc + (NA_WIN_COLS - 1)
    cs = jnp.clip(qc - NA_WIN_COLS // 2, 0, GRID_W - NA_WIN_COLS)
    ok = (kc >= cs) & (kc < cs + NA_WIN_COLS)
    for hh in range(2):
        acc = jnp.zeros(shape, F32)
        for d in range(RPB_COLS):
            acc = jnp.where(dc == d, rep_ref[hh, d:d + 1, :], acc)
        w_scr[...] = jnp.where(ok, acc, NEG_INF)
        for d0 in range(NA_WIN_ROWS):
            o_ref[d0, 0, hh * GRID_W:(hh + 1) * GRID_W, :] = w_scr[:, d0 * GRID_W:d0 * GRID_W + NA_KEYS]


def _bias_slabs(rpb):
    heads = rpb.shape[0]
    rep = jnp.repeat(jnp.transpose(rpb, (0, 2, 1)), GRID_W, axis=-1)
    return pl.pallas_call(
        _bias_kernel,
        grid=(heads // 2,),
        in_specs=[pl.BlockSpec((2, RPB_COLS, RPB_ROWS * GRID_W), lambda p: (p, 0, 0))],
        out_specs=pl.BlockSpec((NA_WIN_ROWS, 1, 2 * GRID_W, NA_KEYS), lambda p: (0, p, 0, 0)),
        out_shape=jax.ShapeDtypeStruct((NA_WIN_ROWS, heads // 2, 2 * GRID_W, NA_KEYS), F32),
        scratch_shapes=[pltpu.VMEM((GRID_W, RPB_ROWS * GRID_W), F32)],
        compiler_params=_cparams(1),
        name="na_bias_slabs",
    )(rep)


def _norm_modulate(x, g, shift, scale):
    y = x * lax.rsqrt(jnp.mean(x * x, axis=-1, keepdims=True) + EPS) * g
    return y * (1.0 + scale) + shift


def _pre_even_kernel(x_ref, mod_ref, g_ref, w_ref, o_ref):
    h = _norm_modulate(x_ref[0], g_ref[...], mod_ref[0, 0:1, :], mod_ref[0, 1:2, :])
    z = jnp.dot(h.astype(BF16), w_ref[...], preferred_element_type=F32)
    o_ref[0] = z.astype(BF16)


def _pre_odd_kernel(x_ref, mod_ref, g_ref, w_ref, cf_ref, sf_ref, cc_ref, sc_ref, o_ref):
    h = _norm_modulate(x_ref[0], g_ref[...], mod_ref[0, 0:1, :], mod_ref[0, 1:2, :])
    z = jnp.dot(h.astype(BF16), w_ref[...], preferred_element_type=F32)
    tile = x_ref.shape[1]
    steps = tile // ROPE_BLOCK
    first = pl.multiple_of(pl.program_id(1) * steps, steps)
    cc = cc_ref[pl.ds(first, steps), :]
    sc = sc_ref[pl.ds(first, steps), :]
    cf = cf_ref[...]
    sf = sf_ref[...]
    c = jnp.concatenate([cc[a:a + 1] * cf - sc[a:a + 1] * sf for a in range(steps)], axis=0)
    s = jnp.concatenate([sc[a:a + 1] * cf + cc[a:a + 1] * sf for a in range(steps)], axis=0)
    lane = lax.broadcasted_iota(jnp.int32, c.shape, 1)
    s = jnp.where(lane < LANES // 2, -s, s)

    def rope(t):
        return t * c + pltpu.roll(t, LANES // 2, 1) * s

    for blk in range(SWA_WIDTH // LANES):
        q = rope(z[:, blk * LANES:(blk + 1) * LANES])
        o_ref[0, :, blk * LANES:(blk + 1) * LANES] = q.astype(BF16)
    k0 = SWA_WIDTH
    v0 = k0 + SWA_KV_WIDTH
    b0 = v0 + SWA_KV_WIDTH
    c0 = b0 + CONV_WIDTH
    x0 = c0 + CONV_WIDTH
    o_ref[0, :, 512:1024] = z[:, b0:c0].astype(BF16)
    o_ref[0, :, 1024:1536] = (z[:, c0:x0] * z[:, x0:x0 + CONV_WIDTH]).astype(BF16)
    o_ref[0, :, 1536:1664] = rope(z[:, k0:v0]).astype(BF16)
    o_ref[0, :, 1664:1792] = z[:, v0:b0].astype(BF16)


def _pre(x, mod, g, w, rope=None):
    b, length, d = x.shape
    n_in = w.shape[1]
    tile = min(PRE_TILE, length)
    in_specs = [
        pl.BlockSpec((1, tile, d), lambda i, j: (i, j, 0)),
        pl.BlockSpec((1, 6, d), lambda i, j: (i, 0, 0)),
        _const_spec((1, d)),
        _const_spec((d, n_in)),
    ]
    args = [x, mod, g.reshape(1, d), w]
    if rope is None:
        kern, n_out = _pre_even_kernel, n_in
    else:
        kern, n_out = _pre_odd_kernel, ODD_OUT
        assert tile % (8 * ROPE_BLOCK) == 0
        in_specs += [_const_spec(t.shape) for t in rope]
        args += list(rope)
    return pl.pallas_call(
        kern,
        grid=(b, length // tile),
        in_specs=in_specs,
        out_specs=pl.BlockSpec((1, tile, n_out), lambda i, j: (i, j, 0)),
        out_shape=jax.ShapeDtypeStruct((b, length, n_out), BF16),
        compiler_params=_cparams(2),
        name="pre_even" if rope is None else "pre_odd",
    )(*args)


def _interleave(main, side):
    out, done = [], 0
    for i, task in enumerate(main):
        out.append(task)
        want = (i + 1) * len(side) // len(main)
        out.extend(side[done:want])
        done = want
    return out


def _pipelined(produce, consume, items, lookahead):
    pending = {}

    def step(idx):
        if idx < len(items):
            pending[idx] = produce(*items[idx])
        if idx >= lookahead:
            done = idx - lookahead
            consume(*items[done], pending.pop(done))

    return [functools.partial(step, idx) for idx in range(len(items) + lookahead)]


def _even_tasks(q_ref, k_ref, v_ref, bias_ref, up_ref, uc_ref, un_ref, pw_ref, ps_ref, out_ref, ext_ref,
                *, j, n_tiles, rows, length):
    row0 = j * ROW_TILE
    win0 = jnp.clip(row0 - NA_WIN_ROWS // 2, 0, rows - NA_KEY_ROWS)
    lane = lax.broadcasted_iota(jnp.int32, (GRID_W, LANES), 1)
    low = lane < HEAD_DIM
    ones_keys = jnp.ones((NA_KEYS, LANES), BF16)

    koffs, d0s = [], []
    for r in range(ROW_TILE):
        row = row0 + r
        rs = jnp.clip(row - NA_WIN_ROWS // 2, 0, rows - NA_WIN_ROWS)
        koffs.append(pl.multiple_of((rs - win0) * GRID_W, GRID_W))
        d0s.append(rs - row + (NA_WIN_ROWS - 1))

    def scores(r, p):
        cols = slice(p * LANES, (p + 1) * LANES)
        qp = q_ref[0, r * GRID_W:(r + 1) * GRID_W, cols]
        zero = jnp.zeros_like(qp)
        qbd = jnp.concatenate([jnp.where(low, qp, zero), jnp.where(low, zero, qp)], axis=0)
        kp = k_ref[0, pl.ds(koffs[r], NA_KEYS), cols]
        s = lax.dot_general(qbd, kp, (((1,), (1,)), ((), ())), preferred_element_type=F32)
        return s + bias_ref[d0s[r], p]

    def attend(r, p, s):
        cols = slice(p * LANES, (p + 1) * LANES)
        m = jnp.max(s, axis=-1, keepdims=True)
        e = jnp.exp(s - m).astype(BF16)
        vp = jnp.concatenate([v_ref[0, pl.ds(koffs[r], NA_KEYS), cols], ones_keys], axis=1)
        oa = jnp.dot(e, vp, preferred_element_type=F32)
        o = oa[:, :LANES] * (1.0 / oa[:, LANES:])
        out_ref[r * GRID_W:(r + 1) * GRID_W, cols] = jnp.where(low, o[:GRID_W], o[GRID_W:]).astype(BF16)

    first = j == 0
    last = j == n_tiles - 1
    zpad = jnp.zeros((POOL_PAD - POOL_EDGE, 512), BF16)
    prev = up_ref[0]
    nxt = un_ref[0]
    ext_ref[0:POOL_PAD - POOL_EDGE, :] = zpad
    ext_ref[POOL_PAD - POOL_EDGE:POOL_PAD, :] = jnp.where(first, jnp.zeros_like(prev), prev)
    ext_ref[POOL_PAD:POOL_PAD + MIX_TILE, :] = uc_ref[0]
    ext_ref[POOL_PAD + MIX_TILE:POOL_PAD + MIX_TILE + POOL_EDGE, :] = jnp.where(last, jnp.zeros_like(nxt), nxt)
    ext_ref[POOL_PAD + MIX_TILE + POOL_EDGE:, :] = zpad
    band_shape = (POOL_BLK, POOL_BLK + 2 * POOL_PAD)
    delta = (lax.broadcasted_iota(jnp.int32, band_shape, 1)
             - lax.broadcasted_iota(jnp.int32, band_shape, 0) - POOL_PAD)
    bands = [jnp.where((delta >= -(w // 2)) & (delta < w - w // 2), 1.0, 0.0).astype(BF16)
             for w in POOL_WINDOWS]

    def pool_sum(half, gp):
        sums = []
        for g in (2 * gp, 2 * gp + 1):
            cols = slice(g * POOL_GROUP_DIM, (g + 1) * POOL_GROUP_DIM)
            xs = [ext_ref[blk * POOL_BLK:blk * POOL_BLK + POOL_BLK + 2 * POOL_PAD, cols]
                  for blk in (2 * half, 2 * half + 1)]
            sums.append(jnp.dot(bands[g], jnp.concatenate(xs, axis=1), preferred_element_type=F32))
        return sums

    def pool_project(half, gp, sums):
        mixed = []
        for g, acc2 in zip((2 * gp, 2 * gp + 1), sums):
            w = POOL_WINDOWS[g]
            cols = slice(g * POOL_GROUP_DIM, (g + 1) * POOL_GROUP_DIM)
            rows_g = []
            for k, blk in enumerate((2 * half, 2 * half + 1)):
                base = blk * POOL_BLK
                u = ext_ref[base + POOL_PAD:base + POOL_PAD + POOL_BLK, cols].astype(F32)
                t = lax.broadcasted_iota(jnp.int32, (POOL_BLK, 1), 0) + (j * MIX_TILE + base)
                lo = jnp.clip(t - w // 2, 0, length)
                hi = jnp.clip(t - w // 2 + w, 0, length)
                acc = acc2[:, k * POOL_GROUP_DIM:(k + 1) * POOL_GROUP_DIM]
                rows_g.append((acc / (hi - lo).astype(F32) - u).astype(BF16))
            mixed.append(jnp.concatenate(rows_g, axis=0))
        cols2 = slice(2 * gp * POOL_GROUP_DIM, 2 * (gp + 1) * POOL_GROUP_DIM)
        y = jnp.dot(jnp.concatenate(mixed, axis=1), pw_ref[gp], preferred_element_type=F32) * ps_ref[:, cols2]
        base = 2 * half * POOL_BLK
        out_ref[base:base + 2 * POOL_BLK, NA_WIDTH + cols2.start:NA_WIDTH + cols2.stop] = y.astype(BF16)

    units = [(r, p) for r in range(ROW_TILE) for p in range(NA_PAIRS)]
    pools = [(half, gp) for half in range(MIX_TILE // (2 * POOL_BLK)) for gp in range(len(POOL_WINDOWS) // 2)]
    return _interleave(_pipelined(scores, attend, units, NA_LOOKAHEAD),
                       _pipelined(pool_sum, pool_project, pools, POOL_LOOKAHEAD))


SWA_KEYS = 3 * SWA_BLOCK
SWA_WIN = MIX_TILE + 2 * SWA_BLOCK
SWA_GROUP = 4
Q_BLOCKS = MIX_TILE // SWA_BLOCK


def _odd_tasks(q_ref, kv_ref, sink_ref, mask_ref, bg_ref, up_ref, uc_ref, un_ref, cw_ref, cb_ref,
               out_ref, ext_ref, *, j, n_tiles, length):
    t0 = j * MIX_TILE
    win0 = jnp.clip(t0 - SWA_BLOCK, 0, length - SWA_WIN)
    lane = lax.broadcasted_iota(jnp.int32, (SWA_BLOCK, LANES), 1)
    qk_a = (lane // (HEAD_DIM // 2)) % 2 == 0
    v_a = lane < HEAD_DIM
    stack = SWA_GROUP * SWA_BLOCK
    ones_keys = jnp.ones((SWA_KEYS, LANES), BF16)
    koffs, rels = [], []
    for qb in range(Q_BLOCKS):
        qs = t0 + qb * SWA_BLOCK
        ks = jnp.clip(qs - SWA_BLOCK, 0, length - SWA_KEYS)
        koffs.append(pl.multiple_of(ks - win0, SWA_BLOCK))
        rels.append((qs - ks) // SWA_BLOCK)

    def scores(qb, c):
        rows_q = slice(qb * SWA_BLOCK, (qb + 1) * SWA_BLOCK)
        qc = q_ref[0, rows_q, c * LANES:(c + 1) * LANES]
        zero = jnp.zeros_like(qc)
        qst = jnp.concatenate([jnp.where(qk_a, qc, zero), jnp.where(qk_a, zero, qc)], axis=0)
        kp = kv_ref[0, pl.ds(koffs[qb], SWA_KEYS), 0:LANES]
        return lax.dot_general(qst, kp, (((1,), (1,)), ((), ())), preferred_element_type=F32)

    def attend(qb, c, s):
        rows_q = slice(qb * SWA_BLOCK, (qb + 1) * SWA_BLOCK)
        mask = mask_ref[rels[qb]]
        s = jnp.concatenate([s[:SWA_BLOCK] + mask, s[SWA_BLOCK:] + mask], axis=0)
        sink = jnp.concatenate([sink_ref[c * SWA_BLOCK:(c + 1) * SWA_BLOCK],
                                sink_ref[stack + c * SWA_BLOCK:stack + (c + 1) * SWA_BLOCK]], axis=0)
        m = jnp.maximum(jnp.max(s, axis=-1, keepdims=True), sink)
        e = jnp.exp(s - m).astype(BF16)
        vp = jnp.concatenate([kv_ref[0, pl.ds(koffs[qb], SWA_KEYS), LANES:2 * LANES], ones_keys], axis=1)
        oa = jnp.dot(e, vp, preferred_element_type=F32)
        o = oa[:, :LANES] * (1.0 / (oa[:, LANES:] + jnp.exp(sink - m)))
        out_ref[rows_q, c * LANES:(c + 1) * LANES] = jnp.where(v_a, o[:SWA_BLOCK], o[SWA_BLOCK:]).astype(BF16)

    first = j == 0
    last = j == n_tiles - 1
    ext_ref[0:CONV_HALO, :] = jnp.where(first, 0.0, up_ref[0].astype(F32))
    ext_ref[CONV_HALO:CONV_HALO + MIX_TILE, :] = uc_ref[0].astype(F32)
    ext_ref[CONV_HALO + MIX_TILE:, :] = jnp.where(last, 0.0, un_ref[0].astype(F32))

    def conv_chunk(blk):
        rows_c = slice(blk * SWA_BLOCK, (blk + 1) * SWA_BLOCK)
        conv = cb_ref[...]
        for jj in range(3):
            off = CONV_HALO - 1 + jj + blk * SWA_BLOCK
            conv = conv + ext_ref[off:off + SWA_BLOCK, :] * cw_ref[jj:jj + 1, :]
        out_ref[rows_c, SWA_WIDTH:SWA_WIDTH + CONV_WIDTH] = (
            bg_ref[0, rows_c, :].astype(F32) * conv).astype(BF16)

    units = [(qb, c) for qb in range(Q_BLOCKS) for c in range(SWA_GROUP)]
    return _interleave(_pipelined(scores, attend, units, SWA_LOOKAHEAD),
                       [functools.partial(conv_chunk, blk) for blk in range(Q_BLOCKS)])


def _swa_masks():
    q = np.arange(SWA_BLOCK)[:, None]
    k = np.arange(SWA_KEYS)[None, :]
    slabs = [np.where(np.abs(q + rel * SWA_BLOCK - k) <= SWA_WINDOW, 0.0, -np.inf) for rel in range(3)]
    return jnp.asarray(np.stack(slabs), F32)


POST_CHUNK = 256


def _post_tasks(x_ref, mix_ref, mod_ref, g_ref, wo_ref, w1_ref, w3_ref, w2_ref, fg_ref, o_ref, hid_ref,
                *, final):
    st = {"x1": [], "x2": []}
    n_d = D_MODEL // POST_CHUNK
    n_ff = D_FF // POST_CHUNK

    def proj_out(c):
        cols = slice(c * POST_CHUNK, (c + 1) * POST_CHUNK)
        mix = jnp.dot(mix_ref[...], wo_ref[:, cols], preferred_element_type=F32)
        st["x1"].append(x_ref[0, :, cols] + mod_ref[0, 2:3, cols] * mix)

    def norm():
        x1 = jnp.concatenate(st["x1"], axis=1)
        st["x1_full"] = x1
        st["h"] = _norm_modulate(x1, g_ref[...], mod_ref[0, 3:4, :], mod_ref[0, 4:5, :]).astype(BF16)

    def up(c):
        cols = slice(c * POST_CHUNK, (c + 1) * POST_CHUNK)
        a = jnp.dot(st["h"], w1_ref[:, cols], preferred_element_type=F32)
        b = jnp.dot(st["h"], w3_ref[:, cols], preferred_element_type=F32)
        hid_ref[:, cols] = (a * jax.nn.sigmoid(a) * b).astype(BF16)

    def down(c):
        cols = slice(c * POST_CHUNK, (c + 1) * POST_CHUNK)
        f = jnp.dot(hid_ref[...], w2_ref[:, cols], preferred_element_type=F32)
        x2 = st["x1_full"][:, cols] + mod_ref[0, 5:6, cols] * f
        if final:
            st["x2"].append(x2)
        else:
            o_ref[0, :, cols] = x2

    def final_norm():
        x2 = jnp.concatenate(st["x2"], axis=1)
        o_ref[0] = x2 * lax.rsqrt(jnp.mean(x2 * x2, axis=-1, keepdims=True) + EPS) * fg_ref[...]

    tasks = [functools.partial(proj_out, c) for c in range(n_d)] + [norm]
    tasks += [functools.partial(up, c) for c in range(n_ff)]
    tasks += [functools.partial(down, c) for c in range(n_d)]
    if final:
        tasks.append(final_norm)
    return tasks


def _mixpost_kernel(*refs, n_mix, mixer, final, n_tiles, mixer_kw):
    mix_refs = refs[:n_mix]
    (x_ref, mod_ref, g_ref, wo_ref, w1_ref, w3_ref, w2_ref, fg_ref, o_ref,
     mix_new, mix_old, hid_ref, ext_ref) = refs[n_mix:]
    s = pl.program_id(0)
    last = pl.num_programs(0) - 1

    def mixer_tasks():
        return mixer(*mix_refs, mix_new, ext_ref, j=s % n_tiles, n_tiles=n_tiles, **mixer_kw)

    def post_tasks():
        return _post_tasks(x_ref, mix_old, mod_ref, g_ref, wo_ref, w1_ref, w3_ref, w2_ref, fg_ref, o_ref,
                           hid_ref, final=final)

    @pl.when(s == 0)
    def _():
        for task in mixer_tasks():
            task()

    @pl.when(jnp.logical_and(s > 0, s < last))
    def _():
        mix_old[...] = mix_new[...]
        for task in _interleave(post_tasks(), mixer_tasks()):
            task()

    @pl.when(s == last)
    def _():
        mix_old[...] = mix_new[...]
        for task in post_tasks():
            task()


def _mixpost(kind, z, mixer_consts, x, mod, g, wo, ffn, layer, final_g, final):
    w1, w3, w2 = ffn
    b, length, d = x.shape
    assert length % MIX_TILE == 0
    n_tiles = length // MIX_TILE
    total = b * n_tiles

    def mix_tile(s):
        t = jnp.minimum(s, total - 1)
        return t // n_tiles, t % n_tiles

    def post_tile(s):
        t = jnp.maximum(s - 1, 0)
        return t // n_tiles, t % n_tiles

    def tile_spec(width, col_block):
        return pl.BlockSpec((1, MIX_TILE, width), lambda s: (*mix_tile(s), col_block))

    def halo_specs(rows, col_block):
        per_tile = MIX_TILE // rows
        n_halo = length // rows

        def before(s):
            i, j = mix_tile(s)
            return i, jnp.maximum(j * per_tile - 1, 0), col_block

        def after(s):
            i, j = mix_tile(s)
            return i, jnp.minimum((j + 1) * per_tile, n_halo - 1), col_block

        return (pl.BlockSpec((1, rows, 512), before), pl.BlockSpec((1, rows, 512), after))

    if kind == "even":
        rows = length // GRID_W
        assert rows >= NA_KEY_ROWS
        bias, pool_w, pool_scale = mixer_consts

        def win_map(s, col0):
            i, j = mix_tile(s)
            win0 = jnp.clip(j * ROW_TILE - NA_WIN_ROWS // 2, 0, rows - NA_KEY_ROWS)
            return i, pl.multiple_of(win0 * GRID_W, GRID_W), col0

        win = (pl.Element(1), pl.Element(NA_KEY_ROWS * GRID_W), pl.Element(NA_WIDTH))
        u_before, u_after = halo_specs(POOL_EDGE, 3)
        mix_specs = [
            tile_spec(NA_WIDTH, 0),
            pl.BlockSpec(win, lambda s: win_map(s, NA_WIDTH)),
            pl.BlockSpec(win, lambda s: win_map(s, 2 * NA_WIDTH)),
            _const_spec(bias.shape),
            u_before, tile_spec(512, 3), u_after,
            _const_spec(pool_w.shape),
            _const_spec((1, 512)),
        ]
        mix_args = [z, z, z, bias, z, z, z, pool_w, pool_scale.reshape(1, 512)]
        mixer, mixer_kw = _even_tasks, dict(rows=rows, length=length)
        ext = pltpu.VMEM((MIX_TILE + 2 * POOL_PAD, 512), BF16)
    else:
        assert length >= SWA_WIN
        sink_rows, conv_w, conv_b = mixer_consts

        def win_map(s):
            i, j = mix_tile(s)
            start = jnp.clip(j * MIX_TILE - SWA_BLOCK, 0, length - SWA_WIN)
            return i, pl.multiple_of(start, SWA_BLOCK), ODD_OUT - 2 * SWA_KV_WIDTH

        u_before, u_after = halo_specs(CONV_HALO, 2)
        mix_specs = [
            tile_spec(512, 0),
            pl.BlockSpec((pl.Element(1), pl.Element(SWA_WIN), pl.Element(2 * SWA_KV_WIDTH)), win_map),
            _const_spec((2 * SWA_GROUP * SWA_BLOCK, 1)),
            _const_spec((3, SWA_BLOCK, SWA_KEYS)),
            tile_spec(512, 1),
            u_before, tile_spec(512, 2), u_after,
            _const_spec((3, CONV_WIDTH)),
            _const_spec((1, CONV_WIDTH)),
        ]
        mix_args = [z, z, sink_rows, _swa_masks(), z, z, z, z, conv_w, conv_b.reshape(1, CONV_WIDTH)]
        mixer, mixer_kw = _odd_tasks, dict(length=length)
        ext = pltpu.VMEM((MIX_TILE + 2 * CONV_HALO, 512), F32)

    post_specs = [
        pl.BlockSpec((1, MIX_TILE, d), lambda s: (*post_tile(s), 0)),
        pl.BlockSpec((1, 6, d), lambda s: (post_tile(s)[0], 0, 0)),
        _const_spec((1, d)),
        _const_spec(wo.shape),
        _layer_spec(w1.shape, layer),
        _layer_spec(w3.shape, layer),
        _layer_spec(w2.shape, layer),
        _const_spec((1, d)),
    ]
    post_args = [x, mod, g.reshape(1, d), wo, w1, w3, w2, final_g.reshape(1, d)]
    return pl.pallas_call(
        functools.partial(_mixpost_kernel, n_mix=len(mix_specs), mixer=mixer, final=final, n_tiles=n_tiles,
                          mixer_kw=mixer_kw),
        grid=(total + 1,),
        in_specs=mix_specs + post_specs,
        out_specs=pl.BlockSpec((1, MIX_TILE, d), lambda s: (*post_tile(s), 0)),
        out_shape=jax.ShapeDtypeStruct((b, length, d), F32),
        scratch_shapes=[pltpu.VMEM((MIX_TILE, D_MODEL), BF16), pltpu.VMEM((MIX_TILE, D_MODEL), BF16),
                        pltpu.VMEM((MIX_TILE, D_FF), BF16), ext],
        compiler_params=_cparams(1),
        name="mixpost_" + kind,
    )(*mix_args, *post_args)


def _permute_odd_in(w):
    d = w.shape[0]
    half = HEAD_DIM // 2
    q = w[:, :SWA_WIDTH].reshape(d, 2, SWA_GROUP, 2, half)
    q = jnp.transpose(q, (0, 2, 3, 1, 4)).reshape(d, SWA_WIDTH)
    k = w[:, SWA_WIDTH:SWA_WIDTH + SWA_KV_WIDTH].reshape(d, 2, 2, half)
    k = jnp.transpose(k, (0, 2, 1, 3)).reshape(d, SWA_KV_WIDTH)
    return jnp.concatenate([q, k, w[:, SWA_WIDTH + SWA_KV_WIDTH:]], axis=1)


def _permute_odd_out(w):
    d = w.shape[1]
    a = w[:SWA_WIDTH].reshape(2, SWA_GROUP, HEAD_DIM, d)
    a = jnp.transpose(a, (1, 0, 2, 3)).reshape(SWA_WIDTH, d)
    return jnp.concatenate([a, w[SWA_WIDTH:]], axis=0)


def _fold_qk_scale(w_in, q_width):
    scale = HEAD_DIM ** -0.5
    assert np.log2(scale) == round(np.log2(scale))
    col = jnp.arange(w_in.shape[-1]) < q_width
    return w_in * jnp.where(col, scale, 1.0).astype(w_in.dtype)


def _pool_pairs(pw):
    zero = jnp.zeros_like(pw[0])
    return jnp.stack([jnp.block([[pw[2 * i], zero], [zero, pw[2 * i + 1]]]) for i in range(pw.shape[0] // 2)])


def _sink_rows(sink):
    return jnp.repeat(sink.astype(F32), SWA_BLOCK).reshape(2 * SWA_GROUP * SWA_BLOCK, 1)


def _trunk(x, mods, p):
    depth = p["ffn"][0].shape[0]
    for i in range(depth):
        jdx = i // 2
        mod = mods[i]
        if i % 2 == 0:
            kind = "even"
            z = _pre(x, mod, p["norm_g"][i, 0], p["even_w_in"][jdx])
            consts = (p["na_bias"][jdx], p["pool_w"][jdx], p["pool_scale"][jdx])
            wo = p["even_w_out"][jdx]
        else:
            kind = "odd"
            z = _pre(x, mod, p["norm_g"][i, 0], p["odd_w_in"][jdx], rope=p["rope"])
            consts = (p["sink_rows"][jdx], p["conv_w"][jdx], p["conv_b"][jdx])
            wo = p["odd_w_out"][jdx]
        x = _mixpost(kind, z, consts, x, mod, p["norm_g"][i, 1], wo, p["ffn"], i, p["final_g"],
                     final=(i == depth - 1))
    return x


def kernel(x_prompt, x_sample, c_prompt, c_sample, ada_w, ada_b, norm_g, final_g, ffn_w1, ffn_w3, ffn_w2,
           even_w_in, na_rpb, pool_w, pool_scale, even_w_out, odd_w_in, swa_sink, conv_w, conv_b, odd_w_out):
    depth = ada_w.shape[0]
    nb_p, nb_s = c_prompt.shape[0], c_sample.shape[0]
    pad = (-(nb_p + nb_s)) % 8
    c_all = jnp.concatenate([c_prompt, c_sample, jnp.zeros((pad, D_MODEL), F32)], axis=0)
    mod = _ada_mod(c_all, ada_w, ada_b)
    mods_p = [mod[i, :nb_p].reshape(nb_p, 6, D_MODEL) for i in range(depth)]
    mods_s = [mod[i, nb_p:nb_p + nb_s].reshape(nb_s, 6, D_MODEL) for i in range(depth)]

    max_len = max(x_prompt.shape[1], x_sample.shape[1])
    n_even, n_odd = even_w_in.shape[0], odd_w_in.shape[0]
    p = {
        "norm_g": norm_g, "final_g": final_g,
        "ffn": (ffn_w1.astype(BF16), ffn_w3.astype(BF16), ffn_w2.astype(BF16)),
        "even_w_in": [_fold_qk_scale(even_w_in[j], NA_WIDTH).astype(BF16) for j in range(n_even)],
        "even_w_out": [even_w_out[j].astype(BF16) for j in range(n_even)],
        "na_bias": [_bias_slabs(na_rpb[j]) for j in range(n_even)],
        "pool_w": [_pool_pairs(pool_w[j]).astype(BF16) for j in range(n_even)],
        "pool_scale": pool_scale,
        "odd_w_in": [_permute_odd_in(_fold_qk_scale(odd_w_in[j], SWA_WIDTH)).astype(BF16) for j in range(n_odd)],
        "odd_w_out": [_permute_odd_out(odd_w_out[j]).astype(BF16) for j in range(n_odd)],
        "sink_rows": [_sink_rows(swa_sink[j]) for j in range(n_odd)],
        "conv_w": conv_w, "conv_b": conv_b,
        "rope": _rope_tables(max_len),
    }
    return (_trunk(x_prompt, mods_p, p), _trunk(x_sample, mods_s, p))
```

```python
import functools

import numpy as np
import jax
import jax.numpy as jnp
from jax import lax
from jax.experimental import pallas as pl
from jax.experimental.pallas import tpu as pltpu

F32 = jnp.float32
BF16 = jnp.bfloat16

D_MODEL = 1024
HEAD_DIM = 64
LANES = 128
GRID_W = 64
NA_WIN_ROWS = 8
NA_WIN_COLS = 16
NA_WIDTH = 512
NA_PAIRS = NA_WIDTH // LANES
POOL_WINDOWS = (2, 4, 8, 16)
POOL_GROUP_DIM = 128
POOL_BLK = 128
POOL_PAD = 64
POOL_EDGE = 16
CONV_HALO = 8
EVEN_IN = 2048
SWA_WIDTH = 512
SWA_KV_WIDTH = 128
SWA_WINDOW = 128
SWA_BLOCK = 128
CONV_WIDTH = 512
ODD_IN = 2304
ODD_OUT = 1792
D_FF = 2816
ROPE_THETA = 10000.0
EPS = 1e-6
NEG_INF = float("-inf")

ROW_TILE = 8
MIX_TILE = ROW_TILE * GRID_W
NA_KEY_ROWS = ROW_TILE + NA_WIN_ROWS - 1
NA_LOOKAHEAD = 3
SWA_LOOKAHEAD = 2
POOL_LOOKAHEAD = 2
PRE_TILE = 1024
VMEM_LIMIT = 56 * 1024 * 1024


def _cparams(n_axes):
    return pltpu.CompilerParams(
        dimension_semantics=("arbitrary",) * n_axes,
        vmem_limit_bytes=VMEM_LIMIT,
    )


def _const_spec(shape):
    nd = len(shape)
    return pl.BlockSpec(shape, lambda *_: (0,) * nd, pipeline_mode=pl.Buffered(1))


def _layer_spec(shape, layer):
    nd = len(shape) - 1
    return pl.BlockSpec((None,) + tuple(shape[1:]), lambda *_: (layer,) + (0,) * nd,
                        pipeline_mode=pl.Buffered(1))


ADA_TILE = 1536


def _ada_kernel(c_ref, w_ref, b_ref, o_ref):
    c = c_ref[...]
    act = c * jax.nn.sigmoid(c)
    o_ref[0] = jnp.dot(act, w_ref[0], precision=lax.Precision.HIGHEST,
                       preferred_element_type=F32) + b_ref[0]


def _ada_mod(c_all, ada_w, ada_b):
    depth, d, n = ada_w.shape
    rows = c_all.shape[0]
    return pl.pallas_call(
        _ada_kernel,
        grid=(depth, n // ADA_TILE),
        in_specs=[
            pl.BlockSpec((rows, d), lambda l, j: (0, 0)),
            pl.BlockSpec((1, d, ADA_TILE), lambda l, j: (l, 0, j)),
            pl.BlockSpec((1, 1, ADA_TILE), lambda l, j: (l, 0, j)),
        ],
        out_specs=pl.BlockSpec((1, rows, ADA_TILE), lambda l, j: (l, 0, j)),
        out_shape=jax.ShapeDtypeStruct((depth, rows, n), F32),
        compiler_params=_cparams(2),
        name="ada_mod",
    )(c_all, ada_w, ada_b.reshape(depth, 1, n))


ROPE_BLOCK = 128


def _rope_kernel(inv_ref, cf_ref, sf_ref, cc_ref, sc_ref):
    inv = inv_ref[...]
    fine = lax.broadcasted_iota(jnp.int32, cf_ref.shape, 0).astype(F32) * inv
    coarse = (lax.broadcasted_iota(jnp.int32, cc_ref.shape, 0) * ROPE_BLOCK).astype(F32) * inv
    cf_ref[...] = jnp.cos(fine)
    sf_ref[...] = jnp.sin(fine)
    cc_ref[...] = jnp.cos(coarse)
    sc_ref[...] = jnp.sin(coarse)


def _rope_tables(length):
    inv = 1.0 / (ROPE_THETA ** (jnp.arange(0, HEAD_DIM, 2, dtype=F32) / HEAD_DIM))
    inv_lanes = jnp.tile(inv, LANES // (HEAD_DIM // 2)).reshape(1, LANES)
    n_coarse = length // ROPE_BLOCK
    shapes = [(ROPE_BLOCK, LANES)] * 2 + [(n_coarse, LANES)] * 2
    return pl.pallas_call(
        _rope_kernel,
        grid=(1,),
        in_specs=[pl.BlockSpec((1, LANES), lambda i: (0, 0))],
        out_specs=[pl.BlockSpec(s, lambda i: (0, 0)) for s in shapes],
        out_shape=[jax.ShapeDtypeStruct(s, F32) for s in shapes],
        compiler_params=_cparams(1),
        name="rope_tables",
    )(inv_lanes)


RPB_ROWS = 2 * NA_WIN_ROWS - 1
RPB_COLS = 2 * NA_WIN_COLS - 1
NA_KEYS = NA_WIN_ROWS * GRID_W


def _bias_kernel(rep_ref, o_ref, w_scr):
    shape = (GRID_W, RPB_ROWS * GRID_W)
    qc = lax.broadcasted_iota(jnp.int32, shape, 0)
    kc = lax.broadcasted_iota(jnp.int32, shape, 1) % GRID_W
    dc = kc - qc + (NA_WIN_COLS - 1)
    cs = jnp.clip(qc - NA_WIN_COLS // 2, 0, GRID_W - NA_WIN_COLS)
    ok = (kc >= cs) & (kc < cs + NA_WIN_COLS)
    for hh in range(2):
        acc = jnp.zeros(shape, F32)
        for d in range(RPB_COLS):
            acc = jnp.where(dc == d, rep_ref[hh, d:d + 1, :], acc)
        w_scr[...] = jnp.where(ok, acc, NEG_INF)
        for d0 in range(NA_WIN_ROWS):
            o_ref[d0, 0, hh * GRID_W:(hh + 1) * GRID_W, :] = w_scr[:, d0 * GRID_W:d0 * GRID_W + NA_KEYS]


def _bias_slabs(rpb):
    heads = rpb.shape[0]
    rep = jnp.repeat(jnp.transpose(rpb, (0, 2, 1)), GRID_W, axis=-1)
    return pl.pallas_call(
        _bias_kernel,
        grid=(heads // 2,),
        in_specs=[pl.BlockSpec((2, RPB_COLS, RPB_ROWS * GRID_W), lambda p: (p, 0, 0))],
        out_specs=pl.BlockSpec((NA_WIN_ROWS, 1, 2 * GRID_W, NA_KEYS), lambda p: (0, p, 0, 0)),
        out_shape=jax.ShapeDtypeStruct((NA_WIN_ROWS, heads // 2, 2 * GRID_W, NA_KEYS), F32),
        scratch_shapes=[pltpu.VMEM((GRID_W, RPB_ROWS * GRID_W), F32)],
        compiler_params=_cparams(1),
        name="na_bias_slabs",
    )(rep)


def _norm_modulate(x, g, shift, scale):
    y = x * lax.rsqrt(jnp.mean(x * x, axis=-1, keepdims=True) + EPS) * g
    return y * (1.0 + scale) + shift


PRE_SPLIT = 2


def _pre_parts(x_ref):
    rows = x_ref.shape[1] // PRE_SPLIT
    return [slice(k * rows, (k + 1) * rows) for k in range(PRE_SPLIT)]


def _pre_even_kernel(x_ref, mod_ref, g_ref, w_ref, o_ref):
    for rows in _pre_parts(x_ref):
        h = _norm_modulate(x_ref[0, rows, :], g_ref[...], mod_ref[0, 0:1, :], mod_ref[0, 1:2, :])
        z = jnp.dot(h.astype(BF16), w_ref[...], preferred_element_type=F32)
        o_ref[0, rows, :] = z.astype(BF16)


def _pre_odd_kernel(x_ref, mod_ref, g_ref, w_ref, cf_ref, sf_ref, cc_ref, sc_ref, o_ref):
    tile = x_ref.shape[1]
    steps = tile // ROPE_BLOCK
    first = pl.multiple_of(pl.program_id(1) * steps, steps)
    cc = cc_ref[pl.ds(first, steps), :]
    sc = sc_ref[pl.ds(first, steps), :]
    cf = cf_ref[...]
    sf = sf_ref[...]
    k0 = SWA_WIDTH
    v0 = k0 + SWA_KV_WIDTH
    b0 = v0 + SWA_KV_WIDTH
    c0 = b0 + CONV_WIDTH
    x0 = c0 + CONV_WIDTH
    for rows in _pre_parts(x_ref):
        h = _norm_modulate(x_ref[0, rows, :], g_ref[...], mod_ref[0, 0:1, :], mod_ref[0, 1:2, :])
        z = jnp.dot(h.astype(BF16), w_ref[...], preferred_element_type=F32)
        coarse = range(rows.start // ROPE_BLOCK, rows.stop // ROPE_BLOCK)
        c = jnp.concatenate([cc[a:a + 1] * cf - sc[a:a + 1] * sf for a in coarse], axis=0)
        s = jnp.concatenate([sc[a:a + 1] * cf + cc[a:a + 1] * sf for a in coarse], axis=0)
        lane = lax.broadcasted_iota(jnp.int32, c.shape, 1)
        s = jnp.where(lane < LANES // 2, -s, s)

        def rope(t, c=c, s=s):
            return t * c + pltpu.roll(t, LANES // 2, 1) * s

        for blk in range(SWA_WIDTH // LANES):
            q = rope(z[:, blk * LANES:(blk + 1) * LANES])
            o_ref[0, rows, blk * LANES:(blk + 1) * LANES] = q.astype(BF16)
        o_ref[0, rows, 512:1024] = z[:, b0:c0].astype(BF16)
        o_ref[0, rows, 1024:1536] = (z[:, c0:x0] * z[:, x0:x0 + CONV_WIDTH]).astype(BF16)
        o_ref[0, rows, 1536:1664] = rope(z[:, k0:v0]).astype(BF16)
        o_ref[0, rows, 1664:1792] = z[:, v0:b0].astype(BF16)


def _pre(x, mod, g, w, rope=None):
    b, length, d = x.shape
    n_in = w.shape[1]
    tile = min(PRE_TILE, length)
    in_specs = [
        pl.BlockSpec((1, tile, d), lambda i, j: (i, j, 0)),
        pl.BlockSpec((1, 6, d), lambda i, j: (i, 0, 0)),
        _const_spec((1, d)),
        _const_spec((d, n_in)),
    ]
    args = [x, mod, g.reshape(1, d), w]
    if rope is None:
        kern, n_out = _pre_even_kernel, n_in
    else:
        kern, n_out = _pre_odd_kernel, ODD_OUT
        assert tile % (8 * ROPE_BLOCK) == 0
        in_specs += [_const_spec(t.shape) for t in rope]
        args += list(rope)
    return pl.pallas_call(
        kern,
        grid=(b, length // tile),
        in_specs=in_specs,
        out_specs=pl.BlockSpec((1, tile, n_out), lambda i, j: (i, j, 0)),
        out_shape=jax.ShapeDtypeStruct((b, length, n_out), BF16),
        compiler_params=_cparams(2),
        name="pre_even" if rope is None else "pre_odd",
    )(*args)


def _interleave(main, side):
    out, done = [], 0
    for i, task in enumerate(main):
        out.append(task)
        want = (i + 1) * len(side) // len(main)
        out.extend(side[done:want])
        done = want
    return out


def _pipelined(produce, consume, items, lookahead):
    pending = {}

    def step(idx):
        if idx < len(items):
            pending[idx] = produce(*items[idx])
        if idx >= lookahead:
            done = idx - lookahead
            consume(*items[done], pending.pop(done))

    return [functools.partial(step, idx) for idx in range(len(items) + lookahead)]


def _even_tasks(q_ref, k_ref, v_ref, bias_ref, up_ref, uc_ref, un_ref, pw_ref, ps_ref, out_ref, ext_ref,
                *, j, n_tiles, rows, length):
    row0 = j * ROW_TILE
    win0 = jnp.clip(row0 - NA_WIN_ROWS // 2, 0, rows - NA_KEY_ROWS)
    lane = lax.broadcasted_iota(jnp.int32, (GRID_W, LANES), 1)
    low = lane < HEAD_DIM
    ones_keys = jnp.ones((NA_KEYS, LANES), BF16)

    koffs, d0s = [], []
    for r in range(ROW_TILE):
        row = row0 + r
        rs = jnp.clip(row - NA_WIN_ROWS // 2, 0, rows - NA_WIN_ROWS)
        koffs.append(pl.multiple_of((rs - win0) * GRID_W, GRID_W))
        d0s.append(rs - row + (NA_WIN_ROWS - 1))

    def scores(r, p):
        cols = slice(p * LANES, (p + 1) * LANES)
        qp = q_ref[0, r * GRID_W:(r + 1) * GRID_W, cols]
        zero = jnp.zeros_like(qp)
        qbd = jnp.concatenate([jnp.where(low, qp, zero), jnp.where(low, zero, qp)], axis=0)
        kp = k_ref[0, pl.ds(koffs[r], NA_KEYS), cols]
        s = lax.dot_general(qbd, kp, (((1,), (1,)), ((), ())), preferred_element_type=F32)
        return s + bias_ref[d0s[r], p]

    def attend(r, p, s):
        cols = slice(p * LANES, (p + 1) * LANES)
        m = jnp.max(s, axis=-1, keepdims=True)
        e = jnp.exp(s - m).astype(BF16)
        vp = jnp.concatenate([v_ref[0, pl.ds(koffs[r], NA_KEYS), cols], ones_keys], axis=1)
        oa = jnp.dot(e, vp, preferred_element_type=F32)
        o = oa[:, :LANES] * (1.0 / oa[:, LANES:])
        out_ref[r * GRID_W:(r + 1) * GRID_W, cols] = jnp.where(low, o[:GRID_W], o[GRID_W:]).astype(BF16)

    first = j == 0
    last = j == n_tiles - 1
    zpad = jnp.zeros((POOL_PAD - POOL_EDGE, 512), BF16)
    prev = up_ref[0]
    nxt = un_ref[0]
    ext_ref[0:POOL_PAD - POOL_EDGE, :] = zpad
    ext_ref[POOL_PAD - POOL_EDGE:POOL_PAD, :] = jnp.where(first, jnp.zeros_like(prev), prev)
    ext_ref[POOL_PAD:POOL_PAD + MIX_TILE, :] = uc_ref[0]
    ext_ref[POOL_PAD + MIX_TILE:POOL_PAD + MIX_TILE + POOL_EDGE, :] = jnp.where(last, jnp.zeros_like(nxt), nxt)
    ext_ref[POOL_PAD + MIX_TILE + POOL_EDGE:, :] = zpad
    band_shape = (POOL_BLK, POOL_BLK + 2 * POOL_PAD)
    delta = (lax.broadcasted_iota(jnp.int32, band_shape, 1)
             - lax.broadcasted_iota(jnp.int32, band_shape, 0) - POOL_PAD)
    bands = [jnp.where((delta >= -(w // 2)) & (delta < w - w // 2), 1.0, 0.0).astype(BF16)
             for w in POOL_WINDOWS]

    def pool_sum(half, gp):
        sums = []
        for g in (2 * gp, 2 * gp + 1):
            cols = slice(g * POOL_GROUP_DIM, (g + 1) * POOL_GROUP_DIM)
            xs = [ext_ref[blk * POOL_BLK:blk * POOL_BLK + POOL_BLK + 2 * POOL_PAD, cols]
                  for blk in (2 * half, 2 * half + 1)]
            sums.append(jnp.dot(bands[g], jnp.concatenate(xs, axis=1), preferred_element_type=F32))
        return sums

    def pool_project(half, gp, sums):
        mixed = []
        for g, acc2 in zip((2 * gp, 2 * gp + 1), sums):
            w = POOL_WINDOWS[g]
            cols = slice(g * POOL_GROUP_DIM, (g + 1) * POOL_GROUP_DIM)
            rows_g = []
            for k, blk in enumerate((2 * half, 2 * half + 1)):
                base = blk * POOL_BLK
                u = ext_ref[base + POOL_PAD:base + POOL_PAD + POOL_BLK, cols].astype(F32)
                t = lax.broadcasted_iota(jnp.int32, (POOL_BLK, 1), 0) + (j * MIX_TILE + base)
                lo = jnp.clip(t - w // 2, 0, length)
                hi = jnp.clip(t - w // 2 + w, 0, length)
                acc = acc2[:, k * POOL_GROUP_DIM:(k + 1) * POOL_GROUP_DIM]
                rows_g.append((acc / (hi - lo).astype(F32) - u).astype(BF16))
            mixed.append(jnp.concatenate(rows_g, axis=0))
        cols2 = slice(2 * gp * POOL_GROUP_DIM, 2 * (gp + 1) * POOL_GROUP_DIM)
        y = jnp.dot(jnp.concatenate(mixed, axis=1), pw_ref[gp], preferred_element_type=F32) * ps_ref[:, cols2]
        base = 2 * half * POOL_BLK
        out_ref[base:base + 2 * POOL_BLK, NA_WIDTH + cols2.start:NA_WIDTH + cols2.stop] = y.astype(BF16)

    units = [(r, p) for r in range(ROW_TILE) for p in range(NA_PAIRS)]
    pools = [(half, gp) for half in range(MIX_TILE // (2 * POOL_BLK)) for gp in range(len(POOL_WINDOWS) // 2)]
    return _interleave(_pipelined(scores, attend, units, NA_LOOKAHEAD),
                       _pipelined(pool_sum, pool_project, pools, POOL_LOOKAHEAD))


SWA_KEYS = 3 * SWA_BLOCK
SWA_WIN = MIX_TILE + 2 * SWA_BLOCK
SWA_GROUP = 4
Q_BLOCKS = MIX_TILE // SWA_BLOCK


def _odd_tasks(q_ref, kv_ref, sink_ref, mask_ref, bg_ref, up_ref, uc_ref, un_ref, cw_ref, cb_ref,
               out_ref, ext_ref, *, j, n_tiles, length):
    t0 = j * MIX_TILE
    win0 = jnp.clip(t0 - SWA_BLOCK, 0, length - SWA_WIN)
    lane = lax.broadcasted_iota(jnp.int32, (SWA_BLOCK, LANES), 1)
    qk_a = (lane // (HEAD_DIM // 2)) % 2 == 0
    v_a = lane < HEAD_DIM
    stack = SWA_GROUP * SWA_BLOCK
    ones_keys = jnp.ones((SWA_KEYS, LANES), BF16)
    koffs, rels = [], []
    for qb in range(Q_BLOCKS):
        qs = t0 + qb * SWA_BLOCK
        ks = jnp.clip(qs - SWA_BLOCK, 0, length - SWA_KEYS)
        koffs.append(pl.multiple_of(ks - win0, SWA_BLOCK))
        rels.append((qs - ks) // SWA_BLOCK)

    def scores(qb, c):
        rows_q = slice(qb * SWA_BLOCK, (qb + 1) * SWA_BLOCK)
        qc = q_ref[0, rows_q, c * LANES:(c + 1) * LANES]
        zero = jnp.zeros_like(qc)
        qst = jnp.concatenate([jnp.where(qk_a, qc, zero), jnp.where(qk_a, zero, qc)], axis=0)
        kp = kv_ref[0, pl.ds(koffs[qb], SWA_KEYS), 0:LANES]
        return lax.dot_general(qst, kp, (((1,), (1,)), ((), ())), preferred_element_type=F32)

    def attend(qb, c, s):
        rows_q = slice(qb * SWA_BLOCK, (qb + 1) * SWA_BLOCK)
        mask = mask_ref[rels[qb]]
        s = jnp.concatenate([s[:SWA_BLOCK] + mask, s[SWA_BLOCK:] + mask], axis=0)
        sink = jnp.concatenate([sink_ref[c * SWA_BLOCK:(c + 1) * SWA_BLOCK],
                                sink_ref[stack + c * SWA_BLOCK:stack + (c + 1) * SWA_BLOCK]], axis=0)
        m = jnp.maximum(jnp.max(s, axis=-1, keepdims=True), sink)
        e = jnp.exp(s - m).astype(BF16)
        vp = jnp.concatenate([kv_ref[0, pl.ds(koffs[qb], SWA_KEYS), LANES:2 * LANES], ones_keys], axis=1)
        oa = jnp.dot(e, vp, preferred_element_type=F32)
        o = oa[:, :LANES] * (1.0 / (oa[:, LANES:] + jnp.exp(sink - m)))
        out_ref[rows_q, c * LANES:(c + 1) * LANES] = jnp.where(v_a, o[:SWA_BLOCK], o[SWA_BLOCK:]).astype(BF16)

    first = j == 0
    last = j == n_tiles - 1
    ext_ref[0:CONV_HALO, :] = jnp.where(first, 0.0, up_ref[0].astype(F32))
    ext_ref[CONV_HALO:CONV_HALO + MIX_TILE, :] = uc_ref[0].astype(F32)
    ext_ref[CONV_HALO + MIX_TILE:, :] = jnp.where(last, 0.0, un_ref[0].astype(F32))

    def conv_chunk(blk):
        rows_c = slice(blk * SWA_BLOCK, (blk + 1) * SWA_BLOCK)
        conv = cb_ref[...]
        for jj in range(3):
            off = CONV_HALO - 1 + jj + blk * SWA_BLOCK
            conv = conv + ext_ref[off:off + SWA_BLOCK, :] * cw_ref[jj:jj + 1, :]
        out_ref[rows_c, SWA_WIDTH:SWA_WIDTH + CONV_WIDTH] = (
            bg_ref[0, rows_c, :].astype(F32) * conv).astype(BF16)

    units = [(qb, c) for qb in range(Q_BLOCKS) for c in range(SWA_GROUP)]
    return _interleave(_pipelined(scores, attend, units, SWA_LOOKAHEAD),
                       [functools.partial(conv_chunk, blk) for blk in range(Q_BLOCKS)])


def _swa_masks():
    q = np.arange(SWA_BLOCK)[:, None]
    k = np.arange(SWA_KEYS)[None, :]
    slabs = [np.where(np.abs(q + rel * SWA_BLOCK - k) <= SWA_WINDOW, 0.0, -np.inf) for rel in range(3)]
    return jnp.asarray(np.stack(slabs), F32)


POST_CHUNK = 256
FF_CHUNK = 256


def _post_tasks(x_ref, mix_ref, mod_ref, g_ref, wo_ref, w1_ref, w3_ref, w2_ref, fg_ref, o_ref, hid_ref,
                *, final):
    st = {"x1": [], "x2": []}
    n_d = D_MODEL // POST_CHUNK
    n_ff = D_FF // FF_CHUNK

    def proj_out(c):
        cols = slice(c * POST_CHUNK, (c + 1) * POST_CHUNK)
        mix = jnp.dot(mix_ref[...], wo_ref[:, cols], preferred_element_type=F32)
        st["x1"].append(x_ref[0, :, cols] + mod_ref[0, 2:3, cols] * mix)

    def norm():
        x1 = jnp.concatenate(st["x1"], axis=1)
        st["x1_full"] = x1
        st["h"] = _norm_modulate(x1, g_ref[...], mod_ref[0, 3:4, :], mod_ref[0, 4:5, :]).astype(BF16)

    def up(c):
        cols = slice(c * FF_CHUNK, (c + 1) * FF_CHUNK)
        a = jnp.dot(st["h"], w1_ref[:, cols], preferred_element_type=F32)
        b = jnp.dot(st["h"], w3_ref[:, cols], preferred_element_type=F32)
        hid_ref[:, cols] = (a * jax.nn.sigmoid(a) * b).astype(BF16)

    def down(c):
        cols = slice(c * POST_CHUNK, (c + 1) * POST_CHUNK)
        f = jnp.dot(hid_ref[...], w2_ref[:, cols], preferred_element_type=F32)
        x2 = st["x1_full"][:, cols] + mod_ref[0, 5:6, cols] * f
        if final:
            st["x2"].append(x2)
        else:
            o_ref[0, :, cols] = x2

    def final_norm():
        x2 = jnp.concatenate(st["x2"], axis=1)
        o_ref[0] = x2 * lax.rsqrt(jnp.mean(x2 * x2, axis=-1, keepdims=True) + EPS) * fg_ref[...]

    tasks = [functools.partial(proj_out, c) for c in range(n_d)] + [norm]
    tasks += [functools.partial(up, c) for c in range(n_ff)]
    tasks += [functools.partial(down, c) for c in range(n_d)]
    if final:
        tasks.append(final_norm)
    return tasks


def _mixpost_kernel(*refs, n_mix, mixer, final, n_tiles, mixer_kw):
    mix_refs = refs[:n_mix]
    (x_ref, mod_ref, g_ref, wo_ref, w1_ref, w3_ref, w2_ref, fg_ref, o_ref,
     mix_new, mix_old, hid_ref, ext_ref) = refs[n_mix:]
    s = pl.program_id(0)
    last = pl.num_programs(0) - 1

    def mixer_tasks():
        return mixer(*mix_refs, mix_new, ext_ref, j=s % n_tiles, n_tiles=n_tiles, **mixer_kw)

    def post_tasks():
        return _post_tasks(x_ref, mix_old, mod_ref, g_ref, wo_ref, w1_ref, w3_ref, w2_ref, fg_ref, o_ref,
                           hid_ref, final=final)

    @pl.when(s == 0)
    def _():
        for task in mixer_tasks():
            task()

    @pl.when(jnp.logical_and(s > 0, s < last))
    def _():
        mix_old[...] = mix_new[...]
        for task in _interleave(post_tasks(), mixer_tasks()):
            task()

    @pl.when(s == last)
    def _():
        mix_old[...] = mix_new[...]
        for task in post_tasks():
            task()


def _mixpost(kind, z, mixer_consts, x, mod, g, wo, ffn, layer, final_g, final):
    w1, w3, w2 = ffn
    b, length, d = x.shape
    assert length % MIX_TILE == 0
    n_tiles = length // MIX_TILE
    total = b * n_tiles

    def mix_tile(s):
        t = jnp.minimum(s, total - 1)
        return t // n_tiles, t % n_tiles

    def post_tile(s):
        t = jnp.maximum(s - 1, 0)
        return t // n_tiles, t % n_tiles

    def tile_spec(width, col_block):
        return pl.BlockSpec((1, MIX_TILE, width), lambda s: (*mix_tile(s), col_block))

    def halo_specs(rows, col_block):
        per_tile = MIX_TILE // rows
        n_halo = length // rows

        def before(s):
            i, j = mix_tile(s)
            return i, jnp.maximum(j * per_tile - 1, 0), col_block

        def after(s):
            i, j = mix_tile(s)
            return i, jnp.minimum((j + 1) * per_tile, n_halo - 1), col_block

        return (pl.BlockSpec((1, rows, 512), before), pl.BlockSpec((1, rows, 512), after))

    if kind == "even":
        rows = length // GRID_W
        assert rows >= NA_KEY_ROWS
        bias, pool_w, pool_scale = mixer_consts

        def win_map(s, col0):
            i, j = mix_tile(s)
            win0 = jnp.clip(j * ROW_TILE - NA_WIN_ROWS // 2, 0, rows - NA_KEY_ROWS)
            return i, pl.multiple_of(win0 * GRID_W, GRID_W), col0

        win = (pl.Element(1), pl.Element(NA_KEY_ROWS * GRID_W), pl.Element(NA_WIDTH))
        u_before, u_after = halo_specs(POOL_EDGE, 3)
        mix_specs = [
            tile_spec(NA_WIDTH, 0),
            pl.BlockSpec(win, lambda s: win_map(s, NA_WIDTH)),
            pl.BlockSpec(win, lambda s: win_map(s, 2 * NA_WIDTH)),
            _const_spec(bias.shape),
            u_before, tile_spec(512, 3), u_after,
            _const_spec(pool_w.shape),
            _const_spec((1, 512)),
        ]
        mix_args = [z, z, z, bias, z, z, z, pool_w, pool_scale.reshape(1, 512)]
        mixer, mixer_kw = _even_tasks, dict(rows=rows, length=length)
        ext = pltpu.VMEM((MIX_TILE + 2 * POOL_PAD, 512), BF16)
    else:
        assert length >= SWA_WIN
        sink_rows, conv_w, conv_b = mixer_consts

        def win_map(s):
            i, j = mix_tile(s)
            start = jnp.clip(j * MIX_TILE - SWA_BLOCK, 0, length - SWA_WIN)
            return i, pl.multiple_of(start, SWA_BLOCK), ODD_OUT - 2 * SWA_KV_WIDTH

        u_before, u_after = halo_specs(CONV_HALO, 2)
        mix_specs = [
            tile_spec(512, 0),
            pl.BlockSpec((pl.Element(1), pl.Element(SWA_WIN), pl.Element(2 * SWA_KV_WIDTH)), win_map),
            _const_spec((2 * SWA_GROUP * SWA_BLOCK, 1)),
            _const_spec((3, SWA_BLOCK, SWA_KEYS)),
            tile_spec(512, 1),
            u_before, tile_spec(512, 2), u_after,
            _const_spec((3, CONV_WIDTH)),
            _const_spec((1, CONV_WIDTH)),
        ]
        mix_args = [z, z, sink_rows, _swa_masks(), z, z, z, z, conv_w, conv_b.reshape(1, CONV_WIDTH)]
        mixer, mixer_kw = _odd_tasks, dict(length=length)
        ext = pltpu.VMEM((MIX_TILE + 2 * CONV_HALO, 512), F32)

    post_specs = [
        pl.BlockSpec((1, MIX_TILE, d), lambda s: (*post_tile(s), 0)),
        pl.BlockSpec((1, 6, d), lambda s: (post_tile(s)[0], 0, 0)),
        _const_spec((1, d)),
        _const_spec(wo.shape),
        _layer_spec(w1.shape, layer),
        _layer_spec(w3.shape, layer),
        _layer_spec(w2.shape, layer),
        _const_spec((1, d)),
    ]
    post_args = [x, mod, g.reshape(1, d), wo, w1, w3, w2, final_g.reshape(1, d)]
    return pl.pallas_call(
        functools.partial(_mixpost_kernel, n_mix=len(mix_specs), mixer=mixer, final=final, n_tiles=n_tiles,
                          mixer_kw=mixer_kw),
        grid=(total + 1,),
        in_specs=mix_specs + post_specs,
        out_specs=pl.BlockSpec((1, MIX_TILE, d), lambda s: (*post_tile(s), 0)),
        out_shape=jax.ShapeDtypeStruct((b, length, d), F32),
        scratch_shapes=[pltpu.VMEM((MIX_TILE, D_MODEL), BF16), pltpu.VMEM((MIX_TILE, D_MODEL), BF16),
                        pltpu.VMEM((MIX_TILE, D_FF), BF16), ext],
        compiler_params=_cparams(1),
        name="mixpost_" + kind,
    )(*mix_args, *post_args)


def _permute_odd_in(w):
    d = w.shape[0]
    half = HEAD_DIM // 2
    q = w[:, :SWA_WIDTH].reshape(d, 2, SWA_GROUP, 2, half)
    q = jnp.transpose(q, (0, 2, 3, 1, 4)).reshape(d, SWA_WIDTH)
    k = w[:, SWA_WIDTH:SWA_WIDTH + SWA_KV_WIDTH].reshape(d, 2, 2, half)
    k = jnp.transpose(k, (0, 2, 1, 3)).reshape(d, SWA_KV_WIDTH)
    return jnp.concatenate([q, k, w[:, SWA_WIDTH + SWA_KV_WIDTH:]], axis=1)


def _permute_odd_out(w):
    d = w.shape[1]
    a = w[:SWA_WIDTH].reshape(2, SWA_GROUP, HEAD_DIM, d)
    a = jnp.transpose(a, (1, 0, 2, 3)).reshape(SWA_WIDTH, d)
    return jnp.concatenate([a, w[SWA_WIDTH:]], axis=0)


def _fold_qk_scale(w_in, q_width):
    scale = HEAD_DIM ** -0.5
    assert np.log2(scale) == round(np.log2(scale))
    col = jnp.arange(w_in.shape[-1]) < q_width
    return w_in * jnp.where(col, scale, 1.0).astype(w_in.dtype)


def _pool_pairs(pw):
    zero = jnp.zeros_like(pw[0])
    return jnp.stack([jnp.block([[pw[2 * i], zero], [zero, pw[2 * i + 1]]]) for i in range(pw.shape[0] // 2)])


def _sink_rows(sink):
    return jnp.repeat(sink.astype(F32), SWA_BLOCK).reshape(2 * SWA_GROUP * SWA_BLOCK, 1)


def _trunk(x, mods, p):
    depth = p["ffn"][0].shape[0]
    for i in range(depth):
        jdx = i // 2
        mod = mods[i]
        if i % 2 == 0:
            kind = "even"
            z = _pre(x, mod, p["norm_g"][i, 0], p["even_w_in"][jdx])
            consts = (p["na_bias"][jdx], p["pool_w"][jdx], p["pool_scale"][jdx])
            wo = p["even_w_out"][jdx]
        else:
            kind = "odd"
            z = _pre(x, mod, p["norm_g"][i, 0], p["odd_w_in"][jdx], rope=p["rope"])
            consts = (p["sink_rows"][jdx], p["conv_w"][jdx], p["conv_b"][jdx])
            wo = p["odd_w_out"][jdx]
        x = _mixpost(kind, z, consts, x, mod, p["norm_g"][i, 1], wo, p["ffn"], i, p["final_g"],
                     final=(i == depth - 1))
    return x


def kernel(x_prompt, x_sample, c_prompt, c_sample, ada_w, ada_b, norm_g, final_g, ffn_w1, ffn_w3, ffn_w2,
           even_w_in, na_rpb, pool_w, pool_scale, even_w_out, odd_w_in, swa_sink, conv_w, conv_b, odd_w_out):
    depth = ada_w.shape[0]
    nb_p, nb_s = c_prompt.shape[0], c_sample.shape[0]
    pad = (-(nb_p + nb_s)) % 8
    c_all = jnp.concatenate([c_prompt, c_sample, jnp.zeros((pad, D_MODEL), F32)], axis=0)
    mod = _ada_mod(c_all, ada_w, ada_b)
    mods_p = [mod[i, :nb_p].reshape(nb_p, 6, D_MODEL) for i in range(depth)]
    mods_s = [mod[i, nb_p:nb_p + nb_s].reshape(nb_s, 6, D_MODEL) for i in range(depth)]

    max_len = max(x_prompt.shape[1], x_sample.shape[1])
    n_even, n_odd = even_w_in.shape[0], odd_w_in.shape[0]
    p = {
        "norm_g": norm_g, "final_g": final_g,
        "ffn": (ffn_w1.astype(BF16), ffn_w3.astype(BF16), ffn_w2.astype(BF16)),
        "even_w_in": [_fold_qk_scale(even_w_in[j], NA_WIDTH).astype(BF16) for j in range(n_even)],
        "even_w_out": [even_w_out[j].astype(BF16) for j in range(n_even)],
        "na_bias": [_bias_slabs(na_rpb[j]) for j in range(n_even)],
        "pool_w": [_pool_pairs(pool_w[j]).astype(BF16) for j in range(n_even)],
        "pool_scale": pool_scale,
        "odd_w_in": [_permute_odd_in(_fold_qk_scale(odd_w_in[j], SWA_WIDTH)).astype(BF16) for j in range(n_odd)],
        "odd_w_out": [_permute_odd_out(odd_w_out[j]).astype(BF16) for j in range(n_odd)],
        "sink_rows": [_sink_rows(swa_sink[j]) for j in range(n_odd)],
        "conv_w": conv_w, "conv_b": conv_b,
        "rope": _rope_tables(max_len),
    }
    return (_trunk(x_prompt, mods_p, p), _trunk(x_sample, mods_s, p))
```

```python
import functools

import numpy as np
import jax
import jax.numpy as jnp
from jax import lax
from jax.experimental import pallas as pl
from jax.experimental.pallas import tpu as pltpu

F32 = jnp.float32
BF16 = jnp.bfloat16

D_MODEL = 1024
HEAD_DIM = 64
LANES = 128
GRID_W = 64
NA_WIN_ROWS = 8
NA_WIN_COLS = 16
NA_WIDTH = 512
NA_PAIRS = NA_WIDTH // LANES
POOL_WINDOWS = (2, 4, 8, 16)
POOL_GROUP_DIM = 128
POOL_BLK = 128
POOL_PAD = 64
POOL_EDGE = 16
CONV_HALO = 8
EVEN_IN = 2048
SWA_WIDTH = 512
SWA_KV_WIDTH = 128
SWA_WINDOW = 128
SWA_BLOCK = 128
CONV_WIDTH = 512
ODD_IN = 2304
ODD_OUT = 1792
D_FF = 2816
ROPE_THETA = 10000.0
EPS = 1e-6
NEG_INF = float("-inf")

ROW_TILE = 8
MIX_TILE = ROW_TILE * GRID_W
NA_KEY_ROWS = ROW_TILE + NA_WIN_ROWS - 1
NA_LOOKAHEAD = 3
SWA_LOOKAHEAD = 2
POOL_LOOKAHEAD = 2
PRE_TILE = 1024
VMEM_LIMIT = 56 * 1024 * 1024


def _cparams(n_axes):
    return pltpu.CompilerParams(
        dimension_semantics=("arbitrary",) * n_axes,
        vmem_limit_bytes=VMEM_LIMIT,
    )


def _const_spec(shape):
    nd = len(shape)
    return pl.BlockSpec(shape, lambda *_: (0,) * nd, pipeline_mode=pl.Buffered(1))


def _layer_spec(shape, layer):
    nd = len(shape) - 1
    return pl.BlockSpec((None,) + tuple(shape[1:]), lambda *_: (layer,) + (0,) * nd,
                        pipeline_mode=pl.Buffered(1))


ADA_TILE = 1536


def _ada_kernel(c_ref, w_ref, b_ref, o_ref):
    c = c_ref[...]
    act = c * jax.nn.sigmoid(c)
    o_ref[0] = jnp.dot(act.astype(BF16), w_ref[0].astype(BF16), preferred_element_type=F32) + b_ref[0]


def _ada_mod(c_all, ada_w, ada_b):
    depth, d, n = ada_w.shape
    rows = c_all.shape[0]
    return pl.pallas_call(
        _ada_kernel,
        grid=(depth, n // ADA_TILE),
        in_specs=[
            pl.BlockSpec((rows, d), lambda l, j: (0, 0)),
            pl.BlockSpec((1, d, ADA_TILE), lambda l, j: (l, 0, j)),
            pl.BlockSpec((1, 1, ADA_TILE), lambda l, j: (l, 0, j)),
        ],
        out_specs=pl.BlockSpec((1, rows, ADA_TILE), lambda l, j: (l, 0, j)),
        out_shape=jax.ShapeDtypeStruct((depth, rows, n), F32),
        compiler_params=_cparams(2),
        name="ada_mod",
    )(c_all, ada_w, ada_b.reshape(depth, 1, n))


ROPE_BLOCK = 128


def _rope_kernel(inv_ref, cf_ref, sf_ref, cc_ref, sc_ref):
    inv = inv_ref[...]
    fine = lax.broadcasted_iota(jnp.int32, cf_ref.shape, 0).astype(F32) * inv
    coarse = (lax.broadcasted_iota(jnp.int32, cc_ref.shape, 0) * ROPE_BLOCK).astype(F32) * inv
    cf_ref[...] = jnp.cos(fine)
    sf_ref[...] = jnp.sin(fine)
    cc_ref[...] = jnp.cos(coarse)
    sc_ref[...] = jnp.sin(coarse)


def _rope_tables(length):
    inv = 1.0 / (ROPE_THETA ** (jnp.arange(0, HEAD_DIM, 2, dtype=F32) / HEAD_DIM))
    inv_lanes = jnp.tile(inv, LANES // (HEAD_DIM // 2)).reshape(1, LANES)
    n_coarse = length // ROPE_BLOCK
    shapes = [(ROPE_BLOCK, LANES)] * 2 + [(n_coarse, LANES)] * 2
    return pl.pallas_call(
        _rope_kernel,
        grid=(1,),
        in_specs=[pl.BlockSpec((1, LANES), lambda i: (0, 0))],
        out_specs=[pl.BlockSpec(s, lambda i: (0, 0)) for s in shapes],
        out_shape=[jax.ShapeDtypeStruct(s, F32) for s in shapes],
        compiler_params=_cparams(1),
        name="rope_tables",
    )(inv_lanes)


RPB_ROWS = 2 * NA_WIN_ROWS - 1
RPB_COLS = 2 * NA_WIN_COLS - 1
NA_KEYS = NA_WIN_ROWS * GRID_W


def _bias_kernel(rep_ref, o_ref, w_scr):
    shape = (GRID_W, RPB_ROWS * GRID_W)
    qc = lax.broadcasted_iota(jnp.int32, shape, 0)
    kc = lax.broadcasted_iota(jnp.int32, shape, 1) % GRID_W
    dc = kc - qc + (NA_WIN_COLS - 1)
    cs = jnp.clip(qc - NA_WIN_COLS // 2, 0, GRID_W - NA_WIN_COLS)
    ok = (kc >= cs) & (kc < cs + NA_WIN_COLS)
    for hh in range(2):
        acc = jnp.zeros(shape, F32)
        for d in range(RPB_COLS):
            acc = jnp.where(dc == d, rep_ref[hh, d:d + 1, :], acc)
        w_scr[...] = jnp.where(ok, acc, NEG_INF)
        for d0 in range(NA_WIN_ROWS):
            o_ref[d0, 0, hh * GRID_W:(hh + 1) * GRID_W, :] = w_scr[:, d0 * GRID_W:d0 * GRID_W + NA_KEYS]


def _bias_slabs(rpb):
    heads = rpb.shape[0]
    rep = jnp.repeat(jnp.transpose(rpb, (0, 2, 1)), GRID_W, axis=-1)
    return pl.pallas_call(
        _bias_kernel,
        grid=(heads // 2,),
        in_specs=[pl.BlockSpec((2, RPB_COLS, RPB_ROWS * GRID_W), lambda p: (p, 0, 0))],
        out_specs=pl.BlockSpec((NA_WIN_ROWS, 1, 2 * GRID_W, NA_KEYS), lambda p: (0, p, 0, 0)),
        out_shape=jax.ShapeDtypeStruct((NA_WIN_ROWS, heads // 2, 2 * GRID_W, NA_KEYS), F32),
        scratch_shapes=[pltpu.VMEM((GRID_W, RPB_ROWS * GRID_W), F32)],
        compiler_params=_cparams(1),
        name="na_bias_slabs",
    )(rep)


def _norm_modulate(x, g, shift, scale):
    y = x * lax.rsqrt(jnp.mean(x * x, axis=-1, keepdims=True) + EPS) * g
    return y * (1.0 + scale) + shift


PRE_SPLIT = 2


def _pre_parts(x_ref):
    rows = x_ref.shape[1] // PRE_SPLIT
    return [slice(k * rows, (k + 1) * rows) for k in range(PRE_SPLIT)]


def _pre_even_kernel(x_ref, mod_ref, g_ref, w_ref, o_ref):
    for rows in _pre_parts(x_ref):
        h = _norm_modulate(x_ref[0, rows, :], g_ref[...], mod_ref[0, 0:1, :], mod_ref[0, 1:2, :])
        z = jnp.dot(h.astype(BF16), w_ref[...], preferred_element_type=F32)
        o_ref[0, rows, :] = z.astype(BF16)


def _pre_odd_kernel(x_ref, mod_ref, g_ref, w_ref, cf_ref, sf_ref, cc_ref, sc_ref, o_ref):
    tile = x_ref.shape[1]
    steps = tile // ROPE_BLOCK
    first = pl.multiple_of(pl.program_id(1) * steps, steps)
    cc = cc_ref[pl.ds(first, steps), :]
    sc = sc_ref[pl.ds(first, steps), :]
    cf = cf_ref[...]
    sf = sf_ref[...]
    k0 = SWA_WIDTH
    v0 = k0 + SWA_KV_WIDTH
    b0 = v0 + SWA_KV_WIDTH
    c0 = b0 + CONV_WIDTH
    x0 = c0 + CONV_WIDTH
    for rows in _pre_parts(x_ref):
        h = _norm_modulate(x_ref[0, rows, :], g_ref[...], mod_ref[0, 0:1, :], mod_ref[0, 1:2, :])
        z = jnp.dot(h.astype(BF16), w_ref[...], preferred_element_type=F32)
        coarse = range(rows.start // ROPE_BLOCK, rows.stop // ROPE_BLOCK)
        c = jnp.concatenate([cc[a:a + 1] * cf - sc[a:a + 1] * sf for a in coarse], axis=0)
        s = jnp.concatenate([sc[a:a + 1] * cf + cc[a:a + 1] * sf for a in coarse], axis=0)
        lane = lax.broadcasted_iota(jnp.int32, c.shape, 1)
        s = jnp.where(lane < LANES // 2, -s, s)

        def rope(t, c=c, s=s):
            return t * c + pltpu.roll(t, LANES // 2, 1) * s

        for blk in range(SWA_WIDTH // LANES):
            q = rope(z[:, blk * LANES:(blk + 1) * LANES])
            o_ref[0, rows, blk * LANES:(blk + 1) * LANES] = q.astype(BF16)
        o_ref[0, rows, 512:1024] = z[:, b0:c0].astype(BF16)
        o_ref[0, rows, 1024:1536] = (z[:, c0:x0] * z[:, x0:x0 + CONV_WIDTH]).astype(BF16)
        o_ref[0, rows, 1536:1664] = rope(z[:, k0:v0]).astype(BF16)
        o_ref[0, rows, 1664:1792] = z[:, v0:b0].astype(BF16)


def _pre(x, mod, g, w, rope=None):
    b, length, d = x.shape
    n_in = w.shape[1]
    tile = min(PRE_TILE, length)
    in_specs = [
        pl.BlockSpec((1, tile, d), lambda i, j: (i, j, 0)),
        pl.BlockSpec((1, 6, d), lambda i, j: (i, 0, 0)),
        _const_spec((1, d)),
        _const_spec((d, n_in)),
    ]
    args = [x, mod, g.reshape(1, d), w]
    if rope is None:
        kern, n_out = _pre_even_kernel, n_in
    else:
        kern, n_out = _pre_odd_kernel, ODD_OUT
        assert tile % (8 * ROPE_BLOCK) == 0
        in_specs += [_const_spec(t.shape) for t in rope]
        args += list(rope)
    return pl.pallas_call(
        kern,
        grid=(b, length // tile),
        in_specs=in_specs,
        out_specs=pl.BlockSpec((1, tile, n_out), lambda i, j: (i, j, 0)),
        out_shape=jax.ShapeDtypeStruct((b, length, n_out), BF16),
        compiler_params=_cparams(2),
        name="pre_even" if rope is None else "pre_odd",
    )(*args)


def _interleave(main, side):
    out, done = [], 0
    for i, task in enumerate(main):
        out.append(task)
        want = (i + 1) * len(side) // len(main)
        out.extend(side[done:want])
        done = want
    return out


def _pipelined(produce, consume, items, lookahead):
    pending = {}

    def step(idx):
        if idx < len(items):
            pending[idx] = produce(*items[idx])
        if idx >= lookahead:
            done = idx - lookahead
            consume(*items[done], pending.pop(done))

    return [functools.partial(step, idx) for idx in range(len(items) + lookahead)]


def _even_tasks(q_ref, k_ref, v_ref, bias_ref, up_ref, uc_ref, un_ref, pw_ref, ps_ref, out_ref, ext_ref,
                *, j, n_tiles, rows, length):
    row0 = j * ROW_TILE
    win0 = jnp.clip(row0 - NA_WIN_ROWS // 2, 0, rows - NA_KEY_ROWS)
    lane = lax.broadcasted_iota(jnp.int32, (GRID_W, LANES), 1)
    low = lane < HEAD_DIM
    ones_keys = jnp.ones((NA_KEYS, LANES), BF16)

    koffs, d0s = [], []
    for r in range(ROW_TILE):
        row = row0 + r
        rs = jnp.clip(row - NA_WIN_ROWS // 2, 0, rows - NA_WIN_ROWS)
        koffs.append(pl.multiple_of((rs - win0) * GRID_W, GRID_W))
        d0s.append(rs - row + (NA_WIN_ROWS - 1))

    def scores(r, p):
        cols = slice(p * LANES, (p + 1) * LANES)
        qp = q_ref[0, r * GRID_W:(r + 1) * GRID_W, cols]
        zero = jnp.zeros_like(qp)
        qbd = jnp.concatenate([jnp.where(low, qp, zero), jnp.where(low, zero, qp)], axis=0)
        kp = k_ref[0, pl.ds(koffs[r], NA_KEYS), cols]
        s = lax.dot_general(qbd, kp, (((1,), (1,)), ((), ())), preferred_element_type=F32)
        return s + bias_ref[d0s[r], p]

    def attend(r, p, s):
        cols = slice(p * LANES, (p + 1) * LANES)
        m = jnp.max(s, axis=-1, keepdims=True)
        e = jnp.exp(s - m).astype(BF16)
        vp = jnp.concatenate([v_ref[0, pl.ds(koffs[r], NA_KEYS), cols], ones_keys], axis=1)
        oa = jnp.dot(e, vp, preferred_element_type=F32)
        o = oa[:, :LANES] * (1.0 / oa[:, LANES:])
        out_ref[r * GRID_W:(r + 1) * GRID_W, cols] = jnp.where(low, o[:GRID_W], o[GRID_W:]).astype(BF16)

    first = j == 0
    last = j == n_tiles - 1
    zpad = jnp.zeros((POOL_PAD - POOL_EDGE, 512), BF16)
    prev = up_ref[0]
    nxt = un_ref[0]
    ext_ref[0:POOL_PAD - POOL_EDGE, :] = zpad
    ext_ref[POOL_PAD - POOL_EDGE:POOL_PAD, :] = jnp.where(first, jnp.zeros_like(prev), prev)
    ext_ref[POOL_PAD:POOL_PAD + MIX_TILE, :] = uc_ref[0]
    ext_ref[POOL_PAD + MIX_TILE:POOL_PAD + MIX_TILE + POOL_EDGE, :] = jnp.where(last, jnp.zeros_like(nxt), nxt)
    ext_ref[POOL_PAD + MIX_TILE + POOL_EDGE:, :] = zpad
    band_shape = (POOL_BLK, POOL_BLK + 2 * POOL_PAD)
    delta = (lax.broadcasted_iota(jnp.int32, band_shape, 1)
             - lax.broadcasted_iota(jnp.int32, band_shape, 0) - POOL_PAD)
    bands = [jnp.where((delta >= -(w // 2)) & (delta < w - w // 2), 1.0, 0.0).astype(BF16)
             for w in POOL_WINDOWS]

    def pool_sum(half, gp):
        sums = []
        for g in (2 * gp, 2 * gp + 1):
            cols = slice(g * POOL_GROUP_DIM, (g + 1) * POOL_GROUP_DIM)
            xs = [ext_ref[blk * POOL_BLK:blk * POOL_BLK + POOL_BLK + 2 * POOL_PAD, cols]
                  for blk in (2 * half, 2 * half + 1)]
            sums.append(jnp.dot(bands[g], jnp.concatenate(xs, axis=1), preferred_element_type=F32))
        return sums

    def pool_project(half, gp, sums):
        mixed = []
        for g, acc2 in zip((2 * gp, 2 * gp + 1), sums):
            w = POOL_WINDOWS[g]
            cols = slice(g * POOL_GROUP_DIM, (g + 1) * POOL_GROUP_DIM)
            rows_g = []
            for k, blk in enumerate((2 * half, 2 * half + 1)):
                base = blk * POOL_BLK
                u = ext_ref[base + POOL_PAD:base + POOL_PAD + POOL_BLK, cols].astype(F32)
                t = lax.broadcasted_iota(jnp.int32, (POOL_BLK, 1), 0) + (j * MIX_TILE + base)
                lo = jnp.clip(t - w // 2, 0, length)
                hi = jnp.clip(t - w // 2 + w, 0, length)
                acc = acc2[:, k * POOL_GROUP_DIM:(k + 1) * POOL_GROUP_DIM]
                rows_g.append((acc / (hi - lo).astype(F32) - u).astype(BF16))
            mixed.append(jnp.concatenate(rows_g, axis=0))
        cols2 = slice(2 * gp * POOL_GROUP_DIM, 2 * (gp + 1) * POOL_GROUP_DIM)
        y = jnp.dot(jnp.concatenate(mixed, axis=1), pw_ref[gp], preferred_element_type=F32) * ps_ref[:, cols2]
        base = 2 * half * POOL_BLK
        out_ref[base:base + 2 * POOL_BLK, NA_WIDTH + cols2.start:NA_WIDTH + cols2.stop] = y.astype(BF16)

    units = [(r, p) for r in range(ROW_TILE) for p in range(NA_PAIRS)]
    pools = [(half, gp) for half in range(MIX_TILE // (2 * POOL_BLK)) for gp in range(len(POOL_WINDOWS) // 2)]
    return _interleave(_pipelined(scores, attend, units, NA_LOOKAHEAD),
                       _pipelined(pool_sum, pool_project, pools, POOL_LOOKAHEAD))


SWA_KEYS = 3 * SWA_BLOCK
SWA_WIN = MIX_TILE + 2 * SWA_BLOCK
SWA_GROUP = 4
Q_BLOCKS = MIX_TILE // SWA_BLOCK


def _odd_tasks(q_ref, kv_ref, sink_ref, mask_ref, bg_ref, up_ref, uc_ref, un_ref, cw_ref, cb_ref,
               out_ref, ext_ref, *, j, n_tiles, length):
    t0 = j * MIX_TILE
    win0 = jnp.clip(t0 - SWA_BLOCK, 0, length - SWA_WIN)
    lane = lax.broadcasted_iota(jnp.int32, (SWA_BLOCK, LANES), 1)
    qk_a = (lane // (HEAD_DIM // 2)) % 2 == 0
    v_a = lane < HEAD_DIM
    stack = SWA_GROUP * SWA_BLOCK
    ones_keys = jnp.ones((SWA_KEYS, LANES), BF16)
    koffs, rels = [], []
    for qb in range(Q_BLOCKS):
        qs = t0 + qb * SWA_BLOCK
        ks = jnp.clip(qs - SWA_BLOCK, 0, length - SWA_KEYS)
        koffs.append(pl.multiple_of(ks - win0, SWA_BLOCK))
        rels.append((qs - ks) // SWA_BLOCK)

    def scores(qb, c):
        rows_q = slice(qb * SWA_BLOCK, (qb + 1) * SWA_BLOCK)
        qc = q_ref[0, rows_q, c * LANES:(c + 1) * LANES]
        zero = jnp.zeros_like(qc)
        qst = jnp.concatenate([jnp.where(qk_a, qc, zero), jnp.where(qk_a, zero, qc)], axis=0)
        kp = kv_ref[0, pl.ds(koffs[qb], SWA_KEYS), 0:LANES]
        return lax.dot_general(qst, kp, (((1,), (1,)), ((), ())), preferred_element_type=F32)

    def attend(qb, c, s):
        rows_q = slice(qb * SWA_BLOCK, (qb + 1) * SWA_BLOCK)
        mask = mask_ref[rels[qb]]
        s = jnp.concatenate([s[:SWA_BLOCK] + mask, s[SWA_BLOCK:] + mask], axis=0)
        sink = jnp.concatenate([sink_ref[c * SWA_BLOCK:(c + 1) * SWA_BLOCK],
                                sink_ref[stack + c * SWA_BLOCK:stack + (c + 1) * SWA_BLOCK]], axis=0)
        m = jnp.maximum(jnp.max(s, axis=-1, keepdims=True), sink)
        e = jnp.exp(s - m).astype(BF16)
        vp = jnp.concatenate([kv_ref[0, pl.ds(koffs[qb], SWA_KEYS), LANES:2 * LANES], ones_keys], axis=1)
        oa = jnp.dot(e, vp, preferred_element_type=F32)
        o = oa[:, :LANES] * (1.0 / (oa[:, LANES:] + jnp.exp(sink - m)))
        out_ref[rows_q, c * LANES:(c + 1) * LANES] = jnp.where(v_a, o[:SWA_BLOCK], o[SWA_BLOCK:]).astype(BF16)

    first = j == 0
    last = j == n_tiles - 1
    ext_ref[0:CONV_HALO, :] = jnp.where(first, 0.0, up_ref[0].astype(F32))
    ext_ref[CONV_HALO:CONV_HALO + MIX_TILE, :] = uc_ref[0].astype(F32)
    ext_ref[CONV_HALO + MIX_TILE:, :] = jnp.where(last, 0.0, un_ref[0].astype(F32))

    def conv_chunk(blk):
        rows_c = slice(blk * SWA_BLOCK, (blk + 1) * SWA_BLOCK)
        conv = cb_ref[...]
        for jj in range(3):
            off = CONV_HALO - 1 + jj + blk * SWA_BLOCK
            conv = conv + ext_ref[off:off + SWA_BLOCK, :] * cw_ref[jj:jj + 1, :]
        out_ref[rows_c, SWA_WIDTH:SWA_WIDTH + CONV_WIDTH] = (
            bg_ref[0, rows_c, :].astype(F32) * conv).astype(BF16)

    units = [(qb, c) for qb in range(Q_BLOCKS) for c in range(SWA_GROUP)]
    return _interleave(_pipelined(scores, attend, units, SWA_LOOKAHEAD),
                       [functools.partial(conv_chunk, blk) for blk in range(Q_BLOCKS)])


def _swa_masks():
    q = np.arange(SWA_BLOCK)[:, None]
    k = np.arange(SWA_KEYS)[None, :]
    slabs = [np.where(np.abs(q + rel * SWA_BLOCK - k) <= SWA_WINDOW, 0.0, -np.inf) for rel in range(3)]
    return jnp.asarray(np.stack(slabs), F32)


POST_CHUNK = 256
FF_CHUNK = 256


def _post_tasks(x_ref, mix_ref, mod_ref, g_ref, wo_ref, w1_ref, w3_ref, w2_ref, fg_ref, o_ref, hid_ref,
                *, final):
    st = {"x1": [], "x2": []}
    n_d = D_MODEL // POST_CHUNK
    n_ff = D_FF // FF_CHUNK

    def proj_out(c):
        cols = slice(c * POST_CHUNK, (c + 1) * POST_CHUNK)
        mix = jnp.dot(mix_ref[...], wo_ref[:, cols], preferred_element_type=F32)
        st["x1"].append(x_ref[0, :, cols] + mod_ref[0, 2:3, cols] * mix)

    def norm():
        x1 = jnp.concatenate(st["x1"], axis=1)
        st["x1_full"] = x1
        st["h"] = _norm_modulate(x1, g_ref[...], mod_ref[0, 3:4, :], mod_ref[0, 4:5, :]).astype(BF16)

    def up(c):
        cols = slice(c * FF_CHUNK, (c + 1) * FF_CHUNK)
        a = jnp.dot(st["h"], w1_ref[:, cols], preferred_element_type=F32)
        b = jnp.dot(st["h"], w3_ref[:, cols], preferred_element_type=F32)
        hid_ref[:, cols] = (a * jax.nn.sigmoid(a) * b).astype(BF16)

    def down(c):
        cols = slice(c * POST_CHUNK, (c + 1) * POST_CHUNK)
        f = jnp.dot(hid_ref[...], w2_ref[:, cols], preferred_element_type=F32)
        x2 = st["x1_full"][:, cols] + mod_ref[0, 5:6, cols] * f
        if final:
            st["x2"].append(x2)
        else:
            o_ref[0, :, cols] = x2

    def final_norm():
        x2 = jnp.concatenate(st["x2"], axis=1)
        o_ref[0] = x2 * lax.rsqrt(jnp.mean(x2 * x2, axis=-1, keepdims=True) + EPS) * fg_ref[...]

    tasks = [functools.partial(proj_out, c) for c in range(n_d)] + [norm]
    tasks += [functools.partial(up, c) for c in range(n_ff)]
    tasks += [functools.partial(down, c) for c in range(n_d)]
    if final:
        tasks.append(final_norm)
    return tasks


def _mixpost_kernel(*refs, n_mix, mixer, final, n_tiles, mixer_kw):
    mix_refs = refs[:n_mix]
    (x_ref, mod_ref, g_ref, wo_ref, w1_ref, w3_ref, w2_ref, fg_ref, o_ref,
     mix_new, mix_old, hid_ref, ext_ref) = refs[n_mix:]
    s = pl.program_id(0)
    last = pl.num_programs(0) - 1

    def mixer_tasks():
        return mixer(*mix_refs, mix_new, ext_ref, j=s % n_tiles, n_tiles=n_tiles, **mixer_kw)

    def post_tasks():
        return _post_tasks(x_ref, mix_old, mod_ref, g_ref, wo_ref, w1_ref, w3_ref, w2_ref, fg_ref, o_ref,
                           hid_ref, final=final)

    @pl.when(s == 0)
    def _():
        for task in mixer_tasks():
            task()

    @pl.when(jnp.logical_and(s > 0, s < last))
    def _():
        mix_old[...] = mix_new[...]
        for task in _interleave(post_tasks(), mixer_tasks()):
            task()

    @pl.when(s == last)
    def _():
        mix_old[...] = mix_new[...]
        for task in post_tasks():
            task()


def _mixpost(kind, z, mixer_consts, x, mod, g, wo, ffn, layer, final_g, final):
    w1, w3, w2 = ffn
    b, length, d = x.shape
    assert length % MIX_TILE == 0
    n_tiles = length // MIX_TILE
    total = b * n_tiles

    def mix_tile(s):
        t = jnp.minimum(s, total - 1)
        return t // n_tiles, t % n_tiles

    def post_tile(s):
        t = jnp.maximum(s - 1, 0)
        return t // n_tiles, t % n_tiles

    def tile_spec(width, col_block):
        return pl.BlockSpec((1, MIX_TILE, width), lambda s: (*mix_tile(s), col_block))

    def halo_specs(rows, col_block):
        per_tile = MIX_TILE // rows
        n_halo = length // rows

        def before(s):
            i, j = mix_tile(s)
            return i, jnp.maximum(j * per_tile - 1, 0), col_block

        def after(s):
            i, j = mix_tile(s)
            return i, jnp.minimum((j + 1) * per_tile, n_halo - 1), col_block

        return (pl.BlockSpec((1, rows, 512), before), pl.BlockSpec((1, rows, 512), after))

    if kind == "even":
        rows = length // GRID_W
        assert rows >= NA_KEY_ROWS
        bias, pool_w, pool_scale = mixer_consts

        def win_map(s, col0):
            i, j = mix_tile(s)
            win0 = jnp.clip(j * ROW_TILE - NA_WIN_ROWS // 2, 0, rows - NA_KEY_ROWS)
            return i, pl.multiple_of(win0 * GRID_W, GRID_W), col0

        win = (pl.Element(1), pl.Element(NA_KEY_ROWS * GRID_W), pl.Element(NA_WIDTH))
        u_before, u_after = halo_specs(POOL_EDGE, 3)
        mix_specs = [
            tile_spec(NA_WIDTH, 0),
            pl.BlockSpec(win, lambda s: win_map(s, NA_WIDTH)),
            pl.BlockSpec(win, lambda s: win_map(s, 2 * NA_WIDTH)),
            _const_spec(bias.shape),
            u_before, tile_spec(512, 3), u_after,
            _const_spec(pool_w.shape),
            _const_spec((1, 512)),
        ]
        mix_args = [z, z, z, bias, z, z, z, pool_w, pool_scale.reshape(1, 512)]
        mixer, mixer_kw = _even_tasks, dict(rows=rows, length=length)
        ext = pltpu.VMEM((MIX_TILE + 2 * POOL_PAD, 512), BF16)
    else:
        assert length >= SWA_WIN
        sink_rows, conv_w, conv_b = mixer_consts

        def win_map(s):
            i, j = mix_tile(s)
            start = jnp.clip(j * MIX_TILE - SWA_BLOCK, 0, length - SWA_WIN)
            return i, pl.multiple_of(start, SWA_BLOCK), ODD_OUT - 2 * SWA_KV_WIDTH

        u_before, u_after = halo_specs(CONV_HALO, 2)
        mix_specs = [
            tile_spec(512, 0),
            pl.BlockSpec((pl.Element(1), pl.Element(SWA_WIN), pl.Element(2 * SWA_KV_WIDTH)), win_map),
            _const_spec((2 * SWA_GROUP * SWA_BLOCK, 1)),
            _const_spec((3, SWA_BLOCK, SWA_KEYS)),
            tile_spec(512, 1),
            u_before, tile_spec(512, 2), u_after,
            _const_spec((3, CONV_WIDTH)),
            _const_spec((1, CONV_WIDTH)),
        ]
        mix_args = [z, z, sink_rows, _swa_masks(), z, z, z, z, conv_w, conv_b.reshape(1, CONV_WIDTH)]
        mixer, mixer_kw = _odd_tasks, dict(length=length)
        ext = pltpu.VMEM((MIX_TILE + 2 * CONV_HALO, 512), F32)

    post_specs = [
        pl.BlockSpec((1, MIX_TILE, d), lambda s: (*post_tile(s), 0)),
        pl.BlockSpec((1, 6, d), lambda s: (post_tile(s)[0], 0, 0)),
        _const_spec((1, d)),
        _const_spec(wo.shape),
        _layer_spec(w1.shape, layer),
        _layer_spec(w3.shape, layer),
        _layer_spec(w2.shape, layer),
        _const_spec((1, d)),
    ]
    post_args = [x, mod, g.reshape(1, d), wo, w1, w3, w2, final_g.reshape(1, d)]
    return pl.pallas_call(
        functools.partial(_mixpost_kernel, n_mix=len(mix_specs), mixer=mixer, final=final, n_tiles=n_tiles,
                          mixer_kw=mixer_kw),
        grid=(total + 1,),
        in_specs=mix_specs + post_specs,
        out_specs=pl.BlockSpec((1, MIX_TILE, d), lambda s: (*post_tile(s), 0)),
        out_shape=jax.ShapeDtypeStruct((b, length, d), F32),
        scratch_shapes=[pltpu.VMEM((MIX_TILE, D_MODEL), BF16), pltpu.VMEM((MIX_TILE, D_MODEL), BF16),
                        pltpu.VMEM((MIX_TILE, D_FF), BF16), ext],
        compiler_params=_cparams(1),
        name="mixpost_" + kind,
    )(*mix_args, *post_args)


def _permute_odd_in(w):
    d = w.shape[0]
    half = HEAD_DIM // 2
    q = w[:, :SWA_WIDTH].reshape(d, 2, SWA_GROUP, 2, half)
    q = jnp.transpose(q, (0, 2, 3, 1, 4)).reshape(d, SWA_WIDTH)
    k = w[:, SWA_WIDTH:SWA_WIDTH + SWA_KV_WIDTH].reshape(d, 2, 2, half)
    k = jnp.transpose(k, (0, 2, 1, 3)).reshape(d, SWA_KV_WIDTH)
    return jnp.concatenate([q, k, w[:, SWA_WIDTH + SWA_KV_WIDTH:]], axis=1)


def _permute_odd_out(w):
    d = w.shape[1]
    a = w[:SWA_WIDTH].reshape(2, SWA_GROUP, HEAD_DIM, d)
    a = jnp.transpose(a, (1, 0, 2, 3)).reshape(SWA_WIDTH, d)
    return jnp.concatenate([a, w[SWA_WIDTH:]], axis=0)


def _fold_qk_scale(w_in, q_width):
    scale = HEAD_DIM ** -0.5
    assert np.log2(scale) == round(np.log2(scale))
    col = jnp.arange(w_in.shape[-1]) < q_width
    return w_in * jnp.where(col, scale, 1.0).astype(w_in.dtype)


def _pool_pairs(pw):
    zero = jnp.zeros_like(pw[0])
    return jnp.stack([jnp.block([[pw[2 * i], zero], [zero, pw[2 * i + 1]]]) for i in range(pw.shape[0] // 2)])


def _sink_rows(sink):
    return jnp.repeat(sink.astype(F32), SWA_BLOCK).reshape(2 * SWA_GROUP * SWA_BLOCK, 1)


def _trunk(x, mods, p):
    depth = p["ffn"][0].shape[0]
    for i in range(depth):
        jdx = i // 2
        mod = mods[i]
        if i % 2 == 0:
            kind = "even"
            z = _pre(x, mod, p["norm_g"][i, 0], p["even_w_in"][jdx])
            consts = (p["na_bias"][jdx], p["pool_w"][jdx], p["pool_scale"][jdx])
            wo = p["even_w_out"][jdx]
        else:
            kind = "odd"
            z = _pre(x, mod, p["norm_g"][i, 0], p["odd_w_in"][jdx], rope=p["rope"])
            consts = (p["sink_rows"][jdx], p["conv_w"][jdx], p["conv_b"][jdx])
            wo = p["odd_w_out"][jdx]
        x = _mixpost(kind, z, consts, x, mod, p["norm_g"][i, 1], wo, p["ffn"], i, p["final_g"],
                     final=(i == depth - 1))
    return x


def kernel(x_prompt, x_sample, c_prompt, c_sample, ada_w, ada_b, norm_g, final_g, ffn_w1, ffn_w3, ffn_w2,
           even_w_in, na_rpb, pool_w, pool_scale, even_w_out, odd_w_in, swa_sink, conv_w, conv_b, odd_w_out):
    depth = ada_w.shape[0]
    nb_p, nb_s = c_prompt.shape[0], c_sample.shape[0]
    pad = (-(nb_p + nb_s)) % 8
    c_all = jnp.concatenate([c_prompt, c_sample, jnp.zeros((pad, D_MODEL), F32)], axis=0)
    mod = _ada_mod(c_all, ada_w, ada_b)
    mods_p = [mod[i, :nb_p].reshape(nb_p, 6, D_MODEL) for i in range(depth)]
    mods_s = [mod[i, nb_p:nb_p + nb_s].reshape(nb_s, 6, D_MODEL) for i in range(depth)]

    max_len = max(x_prompt.shape[1], x_sample.shape[1])
    n_even, n_odd = even_w_in.shape[0], odd_w_in.shape[0]
    p = {
        "norm_g": norm_g, "final_g": final_g,
        "ffn": (ffn_w1.astype(BF16), ffn_w3.astype(BF16), ffn_w2.astype(BF16)),
        "even_w_in": [_fold_qk_scale(even_w_in[j], NA_WIDTH).astype(BF16) for j in range(n_even)],
        "even_w_out": [even_w_out[j].astype(BF16) for j in range(n_even)],
        "na_bias": [_bias_slabs(na_rpb[j]) for j in range(n_even)],
        "pool_w": [_pool_pairs(pool_w[j]).astype(BF16) for j in range(n_even)],
        "pool_scale": pool_scale,
        "odd_w_in": [_permute_odd_in(_fold_qk_scale(odd_w_in[j], SWA_WIDTH)).astype(BF16) for j in range(n_odd)],
        "odd_w_out": [_permute_odd_out(odd_w_out[j]).astype(BF16) for j in range(n_odd)],
        "sink_rows": [_sink_rows(swa_sink[j]) for j in range(n_odd)],
        "conv_w": conv_w, "conv_b": conv_b,
        "rope": _rope_tables(max_len),
    }
    return (_trunk(x_prompt, mods_p, p), _trunk(x_sample, mods_s, p))
```

```python
import functools

import numpy as np
import jax
import jax.numpy as jnp
from jax import lax
from jax.experimental import pallas as pl
from jax.experimental.pallas import tpu as pltpu

F32 = jnp.float32
BF16 = jnp.bfloat16

D_MODEL = 1024
HEAD_DIM = 64
LANES = 128
GRID_W = 64
NA_WIN_ROWS = 8
NA_WIN_COLS = 16
NA_WIDTH = 512
NA_PAIRS = NA_WIDTH // LANES
POOL_WINDOWS = (2, 4, 8, 16)
POOL_GROUP_DIM = 128
POOL_BLK = 128
POOL_PAD = 64
POOL_EDGE = 16
CONV_HALO = 8
EVEN_IN = 2048
SWA_WIDTH = 512
SWA_KV_WIDTH = 128
SWA_WINDOW = 128
SWA_BLOCK = 128
CONV_WIDTH = 512
ODD_IN = 2304
ODD_OUT = 1792
D_FF = 2816
ROPE_THETA = 10000.0
EPS = 1e-6
NEG_INF = float("-inf")

ROW_TILE = 8
MIX_TILE = ROW_TILE * GRID_W
NA_KEY_ROWS = ROW_TILE + NA_WIN_ROWS - 1
NA_LOOKAHEAD = 3
SWA_LOOKAHEAD = 2
POOL_LOOKAHEAD = 2
PRE_TILE = 1024
VMEM_LIMIT = 56 * 1024 * 1024


def _cparams(n_axes):
    return pltpu.CompilerParams(
        dimension_semantics=("arbitrary",) * n_axes,
        vmem_limit_bytes=VMEM_LIMIT,
    )


def _const_spec(shape):
    nd = len(shape)
    return pl.BlockSpec(shape, lambda *_: (0,) * nd, pipeline_mode=pl.Buffered(1))


def _layer_spec(shape, layer):
    nd = len(shape) - 1
    return pl.BlockSpec((None,) + tuple(shape[1:]), lambda *_: (layer,) + (0,) * nd,
                        pipeline_mode=pl.Buffered(1))


ADA_TILE = 1536


def _ada_kernel(c_ref, w_ref, b_ref, o_ref):
    c = c_ref[...]
    act = c * jax.nn.sigmoid(c)
    o_ref[0] = jnp.dot(act.astype(BF16), w_ref[0].astype(BF16), preferred_element_type=F32) + b_ref[0]


def _ada_mod(c_all, ada_w, ada_b):
    depth, d, n = ada_w.shape
    rows = c_all.shape[0]
    return pl.pallas_call(
        _ada_kernel,
        grid=(depth, n // ADA_TILE),
        in_specs=[
            pl.BlockSpec((rows, d), lambda l, j: (0, 0)),
            pl.BlockSpec((1, d, ADA_TILE), lambda l, j: (l, 0, j)),
            pl.BlockSpec((1, 1, ADA_TILE), lambda l, j: (l, 0, j)),
        ],
        out_specs=pl.BlockSpec((1, rows, ADA_TILE), lambda l, j: (l, 0, j)),
        out_shape=jax.ShapeDtypeStruct((depth, rows, n), F32),
        compiler_params=_cparams(2),
        name="ada_mod",
    )(c_all, ada_w, ada_b.reshape(depth, 1, n))


ROPE_BLOCK = 128


def _rope_kernel(inv_ref, cf_ref, sf_ref, cc_ref, sc_ref):
    inv = inv_ref[...]
    fine = lax.broadcasted_iota(jnp.int32, cf_ref.shape, 0).astype(F32) * inv
    coarse = (lax.broadcasted_iota(jnp.int32, cc_ref.shape, 0) * ROPE_BLOCK).astype(F32) * inv
    cf_ref[...] = jnp.cos(fine)
    sf_ref[...] = jnp.sin(fine)
    cc_ref[...] = jnp.cos(coarse)
    sc_ref[...] = jnp.sin(coarse)


def _rope_tables(length):
    inv = 1.0 / (ROPE_THETA ** (jnp.arange(0, HEAD_DIM, 2, dtype=F32) / HEAD_DIM))
    inv_lanes = jnp.tile(inv, LANES // (HEAD_DIM // 2)).reshape(1, LANES)
    n_coarse = length // ROPE_BLOCK
    shapes = [(ROPE_BLOCK, LANES)] * 2 + [(n_coarse, LANES)] * 2
    return pl.pallas_call(
        _rope_kernel,
        grid=(1,),
        in_specs=[pl.BlockSpec((1, LANES), lambda i: (0, 0))],
        out_specs=[pl.BlockSpec(s, lambda i: (0, 0)) for s in shapes],
        out_shape=[jax.ShapeDtypeStruct(s, F32) for s in shapes],
        compiler_params=_cparams(1),
        name="rope_tables",
    )(inv_lanes)


RPB_ROWS = 2 * NA_WIN_ROWS - 1
RPB_COLS = 2 * NA_WIN_COLS - 1
NA_KEYS = NA_WIN_ROWS * GRID_W


def _bias_kernel(rep_ref, o_ref, w_scr):
    shape = (GRID_W, RPB_ROWS * GRID_W)
    qc = lax.broadcasted_iota(jnp.int32, shape, 0)
    kc = lax.broadcasted_iota(jnp.int32, shape, 1) % GRID_W
    dc = kc - qc + (NA_WIN_COLS - 1)
    cs = jnp.clip(qc - NA_WIN_COLS // 2, 0, GRID_W - NA_WIN_COLS)
    ok = (kc >= cs) & (kc < cs + NA_WIN_COLS)
    for hh in range(2):
        acc = jnp.zeros(shape, F32)
        for d in range(RPB_COLS):
            acc = jnp.where(dc == d, rep_ref[hh, d:d + 1, :], acc)
        w_scr[...] = jnp.where(ok, acc, NEG_INF)
        for d0 in range(NA_WIN_ROWS):
            o_ref[d0, 0, hh * GRID_W:(hh + 1) * GRID_W, :] = w_scr[:, d0 * GRID_W:d0 * GRID_W + NA_KEYS]


def _bias_slabs(rpb):
    heads = rpb.shape[0]
    rep = jnp.repeat(jnp.transpose(rpb, (0, 2, 1)), GRID_W, axis=-1)
    return pl.pallas_call(
        _bias_kernel,
        grid=(heads // 2,),
        in_specs=[pl.BlockSpec((2, RPB_COLS, RPB_ROWS * GRID_W), lambda p: (p, 0, 0))],
        out_specs=pl.BlockSpec((NA_WIN_ROWS, 1, 2 * GRID_W, NA_KEYS), lambda p: (0, p, 0, 0)),
        out_shape=jax.ShapeDtypeStruct((NA_WIN_ROWS, heads // 2, 2 * GRID_W, NA_KEYS), F32),
        scratch_shapes=[pltpu.VMEM((GRID_W, RPB_ROWS * GRID_W), F32)],
        compiler_params=_cparams(1),
        name="na_bias_slabs",
    )(rep)


def _norm_modulate(x, g, shift, scale):
    y = x * lax.rsqrt(jnp.mean(x * x, axis=-1, keepdims=True) + EPS) * g
    return y * (1.0 + scale) + shift


PRE_SPLIT = 2


def _pre_parts(x_ref):
    rows = x_ref.shape[1] // PRE_SPLIT
    return [slice(k * rows, (k + 1) * rows) for k in range(PRE_SPLIT)]


def _pre_even_kernel(x_ref, mod_ref, g_ref, w_ref, o_ref):
    for rows in _pre_parts(x_ref):
        h = _norm_modulate(x_ref[0, rows, :], g_ref[...], mod_ref[0, 0:1, :], mod_ref[0, 1:2, :])
        z = jnp.dot(h.astype(BF16), w_ref[...], preferred_element_type=F32)
        o_ref[0, rows, :] = z.astype(BF16)


def _pre_odd_kernel(x_ref, mod_ref, g_ref, w_ref, cf_ref, sf_ref, cc_ref, sc_ref, o_ref):
    tile = x_ref.shape[1]
    steps = tile // ROPE_BLOCK
    first = pl.multiple_of(pl.program_id(1) * steps, steps)
    cc = cc_ref[pl.ds(first, steps), :]
    sc = sc_ref[pl.ds(first, steps), :]
    cf = cf_ref[...]
    sf = sf_ref[...]
    k0 = SWA_WIDTH
    v0 = k0 + SWA_KV_WIDTH
    b0 = v0 + SWA_KV_WIDTH
    c0 = b0 + CONV_WIDTH
    x0 = c0 + CONV_WIDTH
    for rows in _pre_parts(x_ref):
        h = _norm_modulate(x_ref[0, rows, :], g_ref[...], mod_ref[0, 0:1, :], mod_ref[0, 1:2, :])
        z = jnp.dot(h.astype(BF16), w_ref[...], preferred_element_type=F32)
        coarse = range(rows.start // ROPE_BLOCK, rows.stop // ROPE_BLOCK)
        c = jnp.concatenate([cc[a:a + 1] * cf - sc[a:a + 1] * sf for a in coarse], axis=0)
        s = jnp.concatenate([sc[a:a + 1] * cf + cc[a:a + 1] * sf for a in coarse], axis=0)
        lane = lax.broadcasted_iota(jnp.int32, c.shape, 1)
        s = jnp.where(lane < LANES // 2, -s, s)

        def rope(t, c=c, s=s):
            return t * c + pltpu.roll(t, LANES // 2, 1) * s

        for blk in range(SWA_WIDTH // LANES):
            q = rope(z[:, blk * LANES:(blk + 1) * LANES])
            o_ref[0, rows, blk * LANES:(blk + 1) * LANES] = q.astype(BF16)
        o_ref[0, rows, 512:1024] = z[:, b0:c0].astype(BF16)
        o_ref[0, rows, 1024:1536] = (z[:, c0:x0] * z[:, x0:x0 + CONV_WIDTH]).astype(BF16)
        o_ref[0, rows, 1536:1664] = rope(z[:, k0:v0]).astype(BF16)
        o_ref[0, rows, 1664:1792] = z[:, v0:b0].astype(BF16)


def _pre(x, mod, g, w, rope=None):
    b, length, d = x.shape
    n_in = w.shape[1]
    tile = min(PRE_TILE, length)
    in_specs = [
        pl.BlockSpec((1, tile, d), lambda i, j: (i, j, 0)),
        pl.BlockSpec((1, 6, d), lambda i, j: (i, 0, 0)),
        _const_spec((1, d)),
        _const_spec((d, n_in)),
    ]
    args = [x, mod, g.reshape(1, d), w]
    if rope is None:
        kern, n_out = _pre_even_kernel, n_in
    else:
        kern, n_out = _pre_odd_kernel, ODD_OUT
        assert tile % (8 * ROPE_BLOCK) == 0
        in_specs += [_const_spec(t.shape) for t in rope]
        args += list(rope)
    return pl.pallas_call(
        kern,
        grid=(b, length // tile),
        in_specs=in_specs,
        out_specs=pl.BlockSpec((1, tile, n_out), lambda i, j: (i, j, 0)),
        out_shape=jax.ShapeDtypeStruct((b, length, n_out), BF16),
        compiler_params=_cparams(2),
        name="pre_even" if rope is None else "pre_odd",
    )(*args)


def _interleave(main, side):
    out, done = [], 0
    for i, task in enumerate(main):
        out.append(task)
        want = (i + 1) * len(side) // len(main)
        out.extend(side[done:want])
        done = want
    return out


def _pipelined(produce, consume, items, lookahead):
    pending = {}

    def step(idx):
        if idx < len(items):
            pending[idx] = produce(*items[idx])
        if idx >= lookahead:
            done = idx - lookahead
            consume(*items[done], pending.pop(done))

    return [functools.partial(step, idx) for idx in range(len(items) + lookahead)]


def _even_tasks(q_ref, k_ref, v_ref, bias_ref, up_ref, uc_ref, un_ref, pw_ref, ps_ref, out_ref, ext_ref,
                *, j, n_tiles, rows, length):
    row0 = j * ROW_TILE
    win0 = jnp.clip(row0 - NA_WIN_ROWS // 2, 0, rows - NA_KEY_ROWS)
    lane = lax.broadcasted_iota(jnp.int32, (GRID_W, LANES), 1)
    low = lane < HEAD_DIM
    ones_keys = jnp.ones((NA_KEYS, LANES), BF16)

    koffs, d0s = [], []
    for r in range(ROW_TILE):
        row = row0 + r
        rs = jnp.clip(row - NA_WIN_ROWS // 2, 0, rows - NA_WIN_ROWS)
        koffs.append(pl.multiple_of((rs - win0) * GRID_W, GRID_W))
        d0s.append(rs - row + (NA_WIN_ROWS - 1))

    def scores(r, p):
        cols = slice(p * LANES, (p + 1) * LANES)
        qp = q_ref[0, r * GRID_W:(r + 1) * GRID_W, cols]
        zero = jnp.zeros_like(qp)
        qbd = jnp.concatenate([jnp.where(low, qp, zero), jnp.where(low, zero, qp)], axis=0)
        kp = k_ref[0, pl.ds(koffs[r], NA_KEYS), cols]
        s = lax.dot_general(qbd, kp, (((1,), (1,)), ((), ())), preferred_element_type=F32)
        return s + bias_ref[d0s[r], p]

    def attend(r, p, s):
        cols = slice(p * LANES, (p + 1) * LANES)
        m = jnp.max(s, axis=-1, keepdims=True)
        e = jnp.exp(s - m).astype(BF16)
        vp = jnp.concatenate([v_ref[0, pl.ds(koffs[r], NA_KEYS), cols], ones_keys], axis=1)
        oa = jnp.dot(e, vp, preferred_element_type=F32)
        o = oa[:, :LANES] * (1.0 / oa[:, LANES:])
        out_ref[r * GRID_W:(r + 1) * GRID_W, cols] = jnp.where(low, o[:GRID_W], o[GRID_W:]).astype(BF16)

    first = j == 0
    last = j == n_tiles - 1
    zpad = jnp.zeros((POOL_PAD - POOL_EDGE, 512), BF16)
    prev = up_ref[0]
    nxt = un_ref[0]
    ext_ref[0:POOL_PAD - POOL_EDGE, :] = zpad
    ext_ref[POOL_PAD - POOL_EDGE:POOL_PAD, :] = jnp.where(first, jnp.zeros_like(prev), prev)
    ext_ref[POOL_PAD:POOL_PAD + MIX_TILE, :] = uc_ref[0]
    ext_ref[POOL_PAD + MIX_TILE:POOL_PAD + MIX_TILE + POOL_EDGE, :] = jnp.where(last, jnp.zeros_like(nxt), nxt)
    ext_ref[POOL_PAD + MIX_TILE + POOL_EDGE:, :] = zpad
    band_shape = (POOL_BLK, POOL_BLK + 2 * POOL_PAD)
    delta = (lax.broadcasted_iota(jnp.int32, band_shape, 1)
             - lax.broadcasted_iota(jnp.int32, band_shape, 0) - POOL_PAD)
    bands = [jnp.where((delta >= -(w // 2)) & (delta < w - w // 2), 1.0, 0.0).astype(BF16)
             for w in POOL_WINDOWS]

    def pool_sum(half, gp):
        sums = []
        for g in (2 * gp, 2 * gp + 1):
            cols = slice(g * POOL_GROUP_DIM, (g + 1) * POOL_GROUP_DIM)
            xs = [ext_ref[blk * POOL_BLK:blk * POOL_BLK + POOL_BLK + 2 * POOL_PAD, cols]
                  for blk in (2 * half, 2 * half + 1)]
            sums.append(jnp.dot(bands[g], jnp.concatenate(xs, axis=1), preferred_element_type=F32))
        return sums

    def pool_project(half, gp, sums):
        mixed = []
        for g, acc2 in zip((2 * gp, 2 * gp + 1), sums):
            w = POOL_WINDOWS[g]
            cols = slice(g * POOL_GROUP_DIM, (g + 1) * POOL_GROUP_DIM)
            rows_g = []
            for k, blk in enumerate((2 * half, 2 * half + 1)):
                base = blk * POOL_BLK
                u = ext_ref[base + POOL_PAD:base + POOL_PAD + POOL_BLK, cols].astype(F32)
                t = lax.broadcasted_iota(jnp.int32, (POOL_BLK, 1), 0) + (j * MIX_TILE + base)
                lo = jnp.clip(t - w // 2, 0, length)
                hi = jnp.clip(t - w // 2 + w, 0, length)
                acc = acc2[:, k * POOL_GROUP_DIM:(k + 1) * POOL_GROUP_DIM]
                rows_g.append((acc / (hi - lo).astype(F32) - u).astype(BF16))
            mixed.append(jnp.concatenate(rows_g, axis=0))
        cols2 = slice(2 * gp * POOL_GROUP_DIM, 2 * (gp + 1) * POOL_GROUP_DIM)
        y = jnp.dot(jnp.concatenate(mixed, axis=1), pw_ref[gp], preferred_element_type=F32) * ps_ref[:, cols2]
        base = 2 * half * POOL_BLK
        out_ref[base:base + 2 * POOL_BLK, NA_WIDTH + cols2.start:NA_WIDTH + cols2.stop] = y.astype(BF16)

    units = [(r, p) for r in range(ROW_TILE) for p in range(NA_PAIRS)]
    pools = [(half, gp) for half in range(MIX_TILE // (2 * POOL_BLK)) for gp in range(len(POOL_WINDOWS) // 2)]
    return _interleave(_pipelined(scores, attend, units, NA_LOOKAHEAD),
                       _pipelined(pool_sum, pool_project, pools, POOL_LOOKAHEAD))


SWA_KEYS = 3 * SWA_BLOCK
SWA_WIN = MIX_TILE + 2 * SWA_BLOCK
SWA_GROUP = 4
Q_BLOCKS = MIX_TILE // SWA_BLOCK


def _odd_tasks(q_ref, kv_ref, sink_ref, mask_ref, bg_ref, up_ref, uc_ref, un_ref, cw_ref, cb_ref,
               out_ref, ext_ref, *, j, n_tiles, length):
    t0 = j * MIX_TILE
    win0 = jnp.clip(t0 - SWA_BLOCK, 0, length - SWA_WIN)
    lane = lax.broadcasted_iota(jnp.int32, (SWA_BLOCK, LANES), 1)
    qk_a = (lane // (HEAD_DIM // 2)) % 2 == 0
    v_rows = lax.broadcasted_iota(jnp.int32, (LANES, SWA_BLOCK), 0) < HEAD_DIM
    koffs, rels = [], []
    for qb in range(Q_BLOCKS):
        qs = t0 + qb * SWA_BLOCK
        ks = jnp.clip(qs - SWA_BLOCK, 0, length - SWA_KEYS)
        koffs.append(pl.multiple_of(ks - win0, SWA_BLOCK))
        rels.append((qs - ks) // SWA_BLOCK)

    def scores(qb, c):
        rows_q = slice(qb * SWA_BLOCK, (qb + 1) * SWA_BLOCK)
        qc = q_ref[0, rows_q, c * LANES:(c + 1) * LANES]
        zero = jnp.zeros_like(qc)
        qst = jnp.concatenate([jnp.where(qk_a, qc, zero), jnp.where(qk_a, zero, qc)], axis=0)
        kp = kv_ref[0, pl.ds(koffs[qb], SWA_KEYS), 0:LANES]
        return lax.dot_general(kp, qst, (((1,), (1,)), ((), ())), preferred_element_type=F32)

    def attend(qb, c, s):
        rows_q = slice(qb * SWA_BLOCK, (qb + 1) * SWA_BLOCK)
        mask = mask_ref[rels[qb]]
        s = jnp.concatenate([s[:, :SWA_BLOCK] + mask, s[:, SWA_BLOCK:] + mask], axis=1)
        sink = sink_ref[c]
        m = jnp.maximum(jnp.max(s, axis=0, keepdims=True), sink)
        e = jnp.exp(s - m)
        den = jnp.sum(e, axis=0, keepdims=True) + jnp.exp(sink - m)
        vp = kv_ref[0, pl.ds(koffs[qb], SWA_KEYS), LANES:2 * LANES]
        ot = lax.dot_general(vp, e.astype(BF16), (((0,), (0,)), ((), ())), preferred_element_type=F32)
        ot = ot * (1.0 / den)
        o = jnp.where(v_rows, ot[:, :SWA_BLOCK], ot[:, SWA_BLOCK:])
        out_ref[rows_q, c * LANES:(c + 1) * LANES] = o.T.astype(BF16)

    first = j == 0
    last = j == n_tiles - 1
    ext_ref[0:CONV_HALO, :] = jnp.where(first, 0.0, up_ref[0].astype(F32))
    ext_ref[CONV_HALO:CONV_HALO + MIX_TILE, :] = uc_ref[0].astype(F32)
    ext_ref[CONV_HALO + MIX_TILE:, :] = jnp.where(last, 0.0, un_ref[0].astype(F32))

    def conv_chunk(blk):
        rows_c = slice(blk * SWA_BLOCK, (blk + 1) * SWA_BLOCK)
        conv = cb_ref[...]
        for jj in range(3):
            off = CONV_HALO - 1 + jj + blk * SWA_BLOCK
            conv = conv + ext_ref[off:off + SWA_BLOCK, :] * cw_ref[jj:jj + 1, :]
        out_ref[rows_c, SWA_WIDTH:SWA_WIDTH + CONV_WIDTH] = (
            bg_ref[0, rows_c, :].astype(F32) * conv).astype(BF16)

    units = [(qb, c) for qb in range(Q_BLOCKS) for c in range(SWA_GROUP)]
    return _interleave(_pipelined(scores, attend, units, SWA_LOOKAHEAD),
                       [functools.partial(conv_chunk, blk) for blk in range(Q_BLOCKS)])


def _swa_masks():
    q = np.arange(SWA_BLOCK)[None, :]
    k = np.arange(SWA_KEYS)[:, None]
    slabs = [np.where(np.abs(q + rel * SWA_BLOCK - k) <= SWA_WINDOW, 0.0, -np.inf) for rel in range(3)]
    return jnp.asarray(np.stack(slabs), F32)


POST_CHUNK = 256
FF_CHUNK = 256


def _post_tasks(x_ref, mix_ref, mod_ref, g_ref, wo_ref, w1_ref, w3_ref, w2_ref, fg_ref, o_ref, hid_ref,
                *, final):
    st = {"x1": [], "x2": []}
    n_d = D_MODEL // POST_CHUNK
    n_ff = D_FF // FF_CHUNK

    def proj_out(c):
        cols = slice(c * POST_CHUNK, (c + 1) * POST_CHUNK)
        mix = jnp.dot(mix_ref[...], wo_ref[:, cols], preferred_element_type=F32)
        st["x1"].append(x_ref[0, :, cols] + mod_ref[0, 2:3, cols] * mix)

    def norm():
        x1 = jnp.concatenate(st["x1"], axis=1)
        st["x1_full"] = x1
        st["h"] = _norm_modulate(x1, g_ref[...], mod_ref[0, 3:4, :], mod_ref[0, 4:5, :]).astype(BF16)

    def up(c):
        cols = slice(c * FF_CHUNK, (c + 1) * FF_CHUNK)
        a = jnp.dot(st["h"], w1_ref[:, cols], preferred_element_type=F32)
        b = jnp.dot(st["h"], w3_ref[:, cols], preferred_element_type=F32)
        hid_ref[:, cols] = (a * jax.nn.sigmoid(a) * b).astype(BF16)

    def down(c):
        cols = slice(c * POST_CHUNK, (c + 1) * POST_CHUNK)
        f = jnp.dot(hid_ref[...], w2_ref[:, cols], preferred_element_type=F32)
        x2 = st["x1_full"][:, cols] + mod_ref[0, 5:6, cols] * f
        if final:
            st["x2"].append(x2)
        else:
            o_ref[0, :, cols] = x2

    def final_norm():
        x2 = jnp.concatenate(st["x2"], axis=1)
        o_ref[0] = x2 * lax.rsqrt(jnp.mean(x2 * x2, axis=-1, keepdims=True) + EPS) * fg_ref[...]

    tasks = [functools.partial(proj_out, c) for c in range(n_d)] + [norm]
    tasks += [functools.partial(up, c) for c in range(n_ff)]
    tasks += [functools.partial(down, c) for c in range(n_d)]
    if final:
        tasks.append(final_norm)
    return tasks


def _mixpost_kernel(*refs, n_mix, mixer, final, n_tiles, mixer_kw):
    mix_refs = refs[:n_mix]
    (x_ref, mod_ref, g_ref, wo_ref, w1_ref, w3_ref, w2_ref, fg_ref, o_ref,
     mix_new, mix_old, hid_ref, ext_ref) = refs[n_mix:]
    s = pl.program_id(0)
    last = pl.num_programs(0) - 1

    def mixer_tasks():
        return mixer(*mix_refs, mix_new, ext_ref, j=s % n_tiles, n_tiles=n_tiles, **mixer_kw)

    def post_tasks():
        return _post_tasks(x_ref, mix_old, mod_ref, g_ref, wo_ref, w1_ref, w3_ref, w2_ref, fg_ref, o_ref,
                           hid_ref, final=final)

    @pl.when(s == 0)
    def _():
        for task in mixer_tasks():
            task()

    @pl.when(jnp.logical_and(s > 0, s < last))
    def _():
        mix_old[...] = mix_new[...]
        for task in _interleave(post_tasks(), mixer_tasks()):
            task()

    @pl.when(s == last)
    def _():
        mix_old[...] = mix_new[...]
        for task in post_tasks():
            task()


def _mixpost(kind, z, mixer_consts, x, mod, g, wo, ffn, layer, final_g, final):
    w1, w3, w2 = ffn
    b, length, d = x.shape
    assert length % MIX_TILE == 0
    n_tiles = length // MIX_TILE
    total = b * n_tiles

    def mix_tile(s):
        t = jnp.minimum(s, total - 1)
        return t // n_tiles, t % n_tiles

    def post_tile(s):
        t = jnp.maximum(s - 1, 0)
        return t // n_tiles, t % n_tiles

    def tile_spec(width, col_block):
        return pl.BlockSpec((1, MIX_TILE, width), lambda s: (*mix_tile(s), col_block))

    def halo_specs(rows, col_block):
        per_tile = MIX_TILE // rows
        n_halo = length // rows

        def before(s):
            i, j = mix_tile(s)
            return i, jnp.maximum(j * per_tile - 1, 0), col_block

        def after(s):
            i, j = mix_tile(s)
            return i, jnp.minimum((j + 1) * per_tile, n_halo - 1), col_block

        return (pl.BlockSpec((1, rows, 512), before), pl.BlockSpec((1, rows, 512), after))

    if kind == "even":
        rows = length // GRID_W
        assert rows >= NA_KEY_ROWS
        bias, pool_w, pool_scale = mixer_consts

        def win_map(s, col0):
            i, j = mix_tile(s)
            win0 = jnp.clip(j * ROW_TILE - NA_WIN_ROWS // 2, 0, rows - NA_KEY_ROWS)
            return i, pl.multiple_of(win0 * GRID_W, GRID_W), col0

        win = (pl.Element(1), pl.Element(NA_KEY_ROWS * GRID_W), pl.Element(NA_WIDTH))
        u_before, u_after = halo_specs(POOL_EDGE, 3)
        mix_specs = [
            tile_spec(NA_WIDTH, 0),
            pl.BlockSpec(win, lambda s: win_map(s, NA_WIDTH)),
            pl.BlockSpec(win, lambda s: win_map(s, 2 * NA_WIDTH)),
            _const_spec(bias.shape),
            u_before, tile_spec(512, 3), u_after,
            _const_spec(pool_w.shape),
            _const_spec((1, 512)),
        ]
        mix_args = [z, z, z, bias, z, z, z, pool_w, pool_scale.reshape(1, 512)]
        mixer, mixer_kw = _even_tasks, dict(rows=rows, length=length)
        ext = pltpu.VMEM((MIX_TILE + 2 * POOL_PAD, 512), BF16)
    else:
        assert length >= SWA_WIN
        sink_rows, conv_w, conv_b = mixer_consts

        def win_map(s):
            i, j = mix_tile(s)
            start = jnp.clip(j * MIX_TILE - SWA_BLOCK, 0, length - SWA_WIN)
            return i, pl.multiple_of(start, SWA_BLOCK), ODD_OUT - 2 * SWA_KV_WIDTH

        u_before, u_after = halo_specs(CONV_HALO, 2)
        mix_specs = [
            tile_spec(512, 0),
            pl.BlockSpec((pl.Element(1), pl.Element(SWA_WIN), pl.Element(2 * SWA_KV_WIDTH)), win_map),
            _const_spec((SWA_GROUP, 1, 2 * SWA_BLOCK)),
            _const_spec((3, SWA_KEYS, SWA_BLOCK)),
            tile_spec(512, 1),
            u_before, tile_spec(512, 2), u_after,
            _const_spec((3, CONV_WIDTH)),
            _const_spec((1, CONV_WIDTH)),
        ]
        mix_args = [z, z, sink_rows, _swa_masks(), z, z, z, z, conv_w, conv_b.reshape(1, CONV_WIDTH)]
        mixer, mixer_kw = _odd_tasks, dict(length=length)
        ext = pltpu.VMEM((MIX_TILE + 2 * CONV_HALO, 512), F32)

    post_specs = [
        pl.BlockSpec((1, MIX_TILE, d), lambda s: (*post_tile(s), 0)),
        pl.BlockSpec((1, 6, d), lambda s: (post_tile(s)[0], 0, 0)),
        _const_spec((1, d)),
        _const_spec(wo.shape),
        _layer_spec(w1.shape, layer),
        _layer_spec(w3.shape, layer),
        _layer_spec(w2.shape, layer),
        _const_spec((1, d)),
    ]
    post_args = [x, mod, g.reshape(1, d), wo, w1, w3, w2, final_g.reshape(1, d)]
    return pl.pallas_call(
        functools.partial(_mixpost_kernel, n_mix=len(mix_specs), mixer=mixer, final=final, n_tiles=n_tiles,
                          mixer_kw=mixer_kw),
        grid=(total + 1,),
        in_specs=mix_specs + post_specs,
        out_specs=pl.BlockSpec((1, MIX_TILE, d), lambda s: (*post_tile(s), 0)),
        out_shape=jax.ShapeDtypeStruct((b, length, d), F32),
        scratch_shapes=[pltpu.VMEM((MIX_TILE, D_MODEL), BF16), pltpu.VMEM((MIX_TILE, D_MODEL), BF16),
                        pltpu.VMEM((MIX_TILE, D_FF), BF16), ext],
        compiler_params=_cparams(1),
        name="mixpost_" + kind,
    )(*mix_args, *post_args)


def _permute_odd_in(w):
    d = w.shape[0]
    half = HEAD_DIM // 2
    q = w[:, :SWA_WIDTH].reshape(d, 2, SWA_GROUP, 2, half)
    q = jnp.transpose(q, (0, 2, 3, 1, 4)).reshape(d, SWA_WIDTH)
    k = w[:, SWA_WIDTH:SWA_WIDTH + SWA_KV_WIDTH].reshape(d, 2, 2, half)
    k = jnp.transpose(k, (0, 2, 1, 3)).reshape(d, SWA_KV_WIDTH)
    return jnp.concatenate([q, k, w[:, SWA_WIDTH + SWA_KV_WIDTH:]], axis=1)


def _permute_odd_out(w):
    d = w.shape[1]
    a = w[:SWA_WIDTH].reshape(2, SWA_GROUP, HEAD_DIM, d)
    a = jnp.transpose(a, (1, 0, 2, 3)).reshape(SWA_WIDTH, d)
    return jnp.concatenate([a, w[SWA_WIDTH:]], axis=0)


def _fold_qk_scale(w_in, q_width):
    scale = HEAD_DIM ** -0.5
    assert np.log2(scale) == round(np.log2(scale))
    col = jnp.arange(w_in.shape[-1]) < q_width
    return w_in * jnp.where(col, scale, 1.0).astype(w_in.dtype)


def _pool_pairs(pw):
    zero = jnp.zeros_like(pw[0])
    return jnp.stack([jnp.block([[pw[2 * i], zero], [zero, pw[2 * i + 1]]]) for i in range(pw.shape[0] // 2)])


def _sink_rows(sink):
    per_head = jnp.repeat(sink.astype(F32).reshape(2, SWA_GROUP, 1), SWA_BLOCK, axis=2)
    return jnp.transpose(per_head, (1, 0, 2)).reshape(SWA_GROUP, 1, 2 * SWA_BLOCK)


def _trunk(x, mods, p):
    depth = p["ffn"][0].shape[0]
    for i in range(depth):
        jdx = i // 2
        mod = mods[i]
        if i % 2 == 0:
            kind = "even"
            z = _pre(x, mod, p["norm_g"][i, 0], p["even_w_in"][jdx])
            consts = (p["na_bias"][jdx], p["pool_w"][jdx], p["pool_scale"][jdx])
            wo = p["even_w_out"][jdx]
        else:
            kind = "odd"
            z = _pre(x, mod, p["norm_g"][i, 0], p["odd_w_in"][jdx], rope=p["rope"])
            consts = (p["sink_rows"][jdx], p["conv_w"][jdx], p["conv_b"][jdx])
            wo = p["odd_w_out"][jdx]
        x = _mixpost(kind, z, consts, x, mod, p["norm_g"][i, 1], wo, p["ffn"], i, p["final_g"],
                     final=(i == depth - 1))
    return x


def kernel(x_prompt, x_sample, c_prompt, c_sample, ada_w, ada_b, norm_g, final_g, ffn_w1, ffn_w3, ffn_w2,
           even_w_in, na_rpb, pool_w, pool_scale, even_w_out, odd_w_in, swa_sink, conv_w, conv_b, odd_w_out):
    depth = ada_w.shape[0]
    nb_p, nb_s = c_prompt.shape[0], c_sample.shape[0]
    pad = (-(nb_p + nb_s)) % 8
    c_all = jnp.concatenate([c_prompt, c_sample, jnp.zeros((pad, D_MODEL), F32)], axis=0)
    mod = _ada_mod(c_all, ada_w, ada_b)
    mods_p = [mod[i, :nb_p].reshape(nb_p, 6, D_MODEL) for i in range(depth)]
    mods_s = [mod[i, nb_p:nb_p + nb_s].reshape(nb_s, 6, D_MODEL) for i in range(depth)]

    max_len = max(x_prompt.shape[1], x_sample.shape[1])
    n_even, n_odd = even_w_in.shape[0], odd_w_in.shape[0]
    p = {
        "norm_g": norm_g, "final_g": final_g,
        "ffn": (ffn_w1.astype(BF16), ffn_w3.astype(BF16), ffn_w2.astype(BF16)),
        "even_w_in": [_fold_qk_scale(even_w_in[j], NA_WIDTH).astype(BF16) for j in range(n_even)],
        "even_w_out": [even_w_out[j].astype(BF16) for j in range(n_even)],
        "na_bias": [_bias_slabs(na_rpb[j]) for j in range(n_even)],
        "pool_w": [_pool_pairs(pool_w[j]).astype(BF16) for j in range(n_even)],
        "pool_scale": pool_scale,
        "odd_w_in": [_permute_odd_in(_fold_qk_scale(odd_w_in[j], SWA_WIDTH)).astype(BF16) for j in range(n_odd)],
        "odd_w_out": [_permute_odd_out(odd_w_out[j]).astype(BF16) for j in range(n_odd)],
        "sink_rows": [_sink_rows(swa_sink[j]) for j in range(n_odd)],
        "conv_w": conv_w, "conv_b": conv_b,
        "rope": _rope_tables(max_len),
    }
    return (_trunk(x_prompt, mods_p, p), _trunk(x_sample, mods_s, p))
```

```python
import functools

import numpy as np
import jax
import jax.numpy as jnp
from jax import lax
from jax.experimental import pallas as pl
from jax.experimental.pallas import tpu as pltpu

F32 = jnp.float32
BF16 = jnp.bfloat16

D_MODEL = 1024
HEAD_DIM = 64
LANES = 128
GRID_W = 64
NA_WIN_ROWS = 8
NA_WIN_COLS = 16
NA_WIDTH = 512
NA_PAIRS = NA_WIDTH // LANES
POOL_WINDOWS = (2, 4, 8, 16)
POOL_GROUP_DIM = 128
POOL_BLK = 128
POOL_PAD = 64
POOL_EDGE = 16
CONV_HALO = 8
EVEN_IN = 2048
SWA_WIDTH = 512
SWA_KV_WIDTH = 128
SWA_WINDOW = 128
SWA_BLOCK = 128
CONV_WIDTH = 512
ODD_IN = 2304
ODD_OUT = 1792
D_FF = 2816
ROPE_THETA = 10000.0
EPS = 1e-6
NEG_INF = float("-inf")

ROW_TILE = 8
MIX_TILE = ROW_TILE * GRID_W
NA_KEY_ROWS = ROW_TILE + NA_WIN_ROWS - 1
NA_LOOKAHEAD = 3
SWA_LOOKAHEAD = 2
POOL_LOOKAHEAD = 2
PRE_TILE = 1024
VMEM_LIMIT = 56 * 1024 * 1024


def _cparams(n_axes):
    return pltpu.CompilerParams(
        dimension_semantics=("arbitrary",) * n_axes,
        vmem_limit_bytes=VMEM_LIMIT,
    )


def _const_spec(shape):
    nd = len(shape)
    return pl.BlockSpec(shape, lambda *_: (0,) * nd, pipeline_mode=pl.Buffered(1))


def _layer_spec(shape, layer):
    nd = len(shape) - 1
    return pl.BlockSpec((None,) + tuple(shape[1:]), lambda *_: (layer,) + (0,) * nd,
                        pipeline_mode=pl.Buffered(1))


ADA_TILE = 3072


def _ada_kernel(c_ref, w_ref, b_ref, o_ref):
    c = c_ref[...]
    act = c * jax.nn.sigmoid(c)
    o_ref[0] = jnp.dot(act.astype(BF16), w_ref[0].astype(BF16), preferred_element_type=F32) + b_ref[0]


def _ada_mod(c_all, ada_w, ada_b):
    depth, d, n = ada_w.shape
    rows = c_all.shape[0]
    return pl.pallas_call(
        _ada_kernel,
        grid=(depth, n // ADA_TILE),
        in_specs=[
            pl.BlockSpec((rows, d), lambda l, j: (0, 0)),
            pl.BlockSpec((1, d, ADA_TILE), lambda l, j: (l, 0, j)),
            pl.BlockSpec((1, 1, ADA_TILE), lambda l, j: (l, 0, j)),
        ],
        out_specs=pl.BlockSpec((1, rows, ADA_TILE), lambda l, j: (l, 0, j)),
        out_shape=jax.ShapeDtypeStruct((depth, rows, n), F32),
        compiler_params=_cparams(2),
        name="ada_mod",
    )(c_all, ada_w, ada_b.reshape(depth, 1, n))


ROPE_BLOCK = 128


def _rope_kernel(inv_ref, cf_ref, sf_ref, cc_ref, sc_ref):
    inv = inv_ref[...]
    fine = lax.broadcasted_iota(jnp.int32, cf_ref.shape, 0).astype(F32) * inv
    coarse = (lax.broadcasted_iota(jnp.int32, cc_ref.shape, 0) * ROPE_BLOCK).astype(F32) * inv
    cf_ref[...] = jnp.cos(fine)
    sf_ref[...] = jnp.sin(fine)
    cc_ref[...] = jnp.cos(coarse)
    sc_ref[...] = jnp.sin(coarse)


def _rope_tables(length):
    inv = 1.0 / (ROPE_THETA ** (jnp.arange(0, HEAD_DIM, 2, dtype=F32) / HEAD_DIM))
    inv_lanes = jnp.tile(inv, LANES // (HEAD_DIM // 2)).reshape(1, LANES)
    n_coarse = length // ROPE_BLOCK
    shapes = [(ROPE_BLOCK, LANES)] * 2 + [(n_coarse, LANES)] * 2
    return pl.pallas_call(
        _rope_kernel,
        grid=(1,),
        in_specs=[pl.BlockSpec((1, LANES), lambda i: (0, 0))],
        out_specs=[pl.BlockSpec(s, lambda i: (0, 0)) for s in shapes],
        out_shape=[jax.ShapeDtypeStruct(s, F32) for s in shapes],
        compiler_params=_cparams(1),
        name="rope_tables",
    )(inv_lanes)


RPB_ROWS = 2 * NA_WIN_ROWS - 1
RPB_COLS = 2 * NA_WIN_COLS - 1
NA_KEYS = NA_WIN_ROWS * GRID_W


def _bias_kernel(rep_ref, o_ref, w_scr):
    shape = (GRID_W, RPB_ROWS * GRID_W)
    qc = lax.broadcasted_iota(jnp.int32, shape, 0)
    kc = lax.broadcasted_iota(jnp.int32, shape, 1) % GRID_W
    dc = kc - qc + (NA_WIN_COLS - 1)
    cs = jnp.clip(qc - NA_WIN_COLS // 2, 0, GRID_W - NA_WIN_COLS)
    ok = (kc >= cs) & (kc < cs + NA_WIN_COLS)
    for hh in range(2):
        acc = jnp.zeros(shape, F32)
        for d in range(RPB_COLS):
            acc = jnp.where(dc == d, rep_ref[hh, d:d + 1, :], acc)
        w_scr[...] = jnp.where(ok, acc, NEG_INF)
        for d0 in range(NA_WIN_ROWS):
            o_ref[d0, 0, hh * GRID_W:(hh + 1) * GRID_W, :] = w_scr[:, d0 * GRID_W:d0 * GRID_W + NA_KEYS]


def _bias_slabs(rpb):
    heads = rpb.shape[0]
    rep = jnp.repeat(jnp.transpose(rpb, (0, 2, 1)), GRID_W, axis=-1)
    return pl.pallas_call(
        _bias_kernel,
        grid=(heads // 2,),
        in_specs=[pl.BlockSpec((2, RPB_COLS, RPB_ROWS * GRID_W), lambda p: (p, 0, 0))],
        out_specs=pl.BlockSpec((NA_WIN_ROWS, 1, 2 * GRID_W, NA_KEYS), lambda p: (0, p, 0, 0)),
        out_shape=jax.ShapeDtypeStruct((NA_WIN_ROWS, heads // 2, 2 * GRID_W, NA_KEYS), F32),
        scratch_shapes=[pltpu.VMEM((GRID_W, RPB_ROWS * GRID_W), F32)],
        compiler_params=_cparams(1),
        name="na_bias_slabs",
    )(rep)


def _norm_modulate(x, g, shift, scale):
    y = x * lax.rsqrt(jnp.mean(x * x, axis=-1, keepdims=True) + EPS) * g
    return y * (1.0 + scale) + shift


PRE_SPLIT = 2


def _pre_parts(x_ref):
    rows = x_ref.shape[1] // PRE_SPLIT
    return [slice(k * rows, (k + 1) * rows) for k in range(PRE_SPLIT)]


def _pre_even_kernel(x_ref, mod_ref, g_ref, w_ref, o_ref):
    for rows in _pre_parts(x_ref):
        h = _norm_modulate(x_ref[0, rows, :], g_ref[...], mod_ref[0, 0:1, :], mod_ref[0, 1:2, :])
        z = jnp.dot(h.astype(BF16), w_ref[...], preferred_element_type=F32)
        o_ref[0, rows, :] = z.astype(BF16)


def _pre_odd_kernel(x_ref, mod_ref, g_ref, w_ref, cf_ref, sf_ref, cc_ref, sc_ref, o_ref):
    tile = x_ref.shape[1]
    steps = tile // ROPE_BLOCK
    first = pl.multiple_of(pl.program_id(1) * steps, steps)
    cc = cc_ref[pl.ds(first, steps), :]
    sc = sc_ref[pl.ds(first, steps), :]
    cf = cf_ref[...]
    sf = sf_ref[...]
    k0 = SWA_WIDTH
    v0 = k0 + SWA_KV_WIDTH
    b0 = v0 + SWA_KV_WIDTH
    c0 = b0 + CONV_WIDTH
    x0 = c0 + CONV_WIDTH
    for rows in _pre_parts(x_ref):
        h = _norm_modulate(x_ref[0, rows, :], g_ref[...], mod_ref[0, 0:1, :], mod_ref[0, 1:2, :])
        z = jnp.dot(h.astype(BF16), w_ref[...], preferred_element_type=F32)
        coarse = range(rows.start // ROPE_BLOCK, rows.stop // ROPE_BLOCK)
        c = jnp.concatenate([cc[a:a + 1] * cf - sc[a:a + 1] * sf for a in coarse], axis=0)
        s = jnp.concatenate([sc[a:a + 1] * cf + cc[a:a + 1] * sf for a in coarse], axis=0)
        lane = lax.broadcasted_iota(jnp.int32, c.shape, 1)
        s = jnp.where(lane < LANES // 2, -s, s)

        def rope(t, c=c, s=s):
            return t * c + pltpu.roll(t, LANES // 2, 1) * s

        for blk in range(SWA_WIDTH // LANES):
            q = rope(z[:, blk * LANES:(blk + 1) * LANES])
            o_ref[0, rows, blk * LANES:(blk + 1) * LANES] = q.astype(BF16)
        o_ref[0, rows, 512:1024] = z[:, b0:c0].astype(BF16)
        o_ref[0, rows, 1024:1536] = (z[:, c0:x0] * z[:, x0:x0 + CONV_WIDTH]).astype(BF16)
        o_ref[0, rows, 1536:1664] = rope(z[:, k0:v0]).astype(BF16)
        o_ref[0, rows, 1664:1792] = z[:, v0:b0].astype(BF16)


def _pre(x, mod, g, w, rope=None):
    b, length, d = x.shape
    n_in = w.shape[1]
    tile = min(PRE_TILE, length)
    in_specs = [
        pl.BlockSpec((1, tile, d), lambda i, j: (i, j, 0)),
        pl.BlockSpec((1, 6, d), lambda i, j: (i, 0, 0)),
        _const_spec((1, d)),
        _const_spec((d, n_in)),
    ]
    args = [x, mod, g.reshape(1, d), w]
    if rope is None:
        kern, n_out = _pre_even_kernel, n_in
    else:
        kern, n_out = _pre_odd_kernel, ODD_OUT
        assert tile % (8 * ROPE_BLOCK) == 0
        in_specs += [_const_spec(t.shape) for t in rope]
        args += list(rope)
    return pl.pallas_call(
        kern,
        grid=(b, length // tile),
        in_specs=in_specs,
        out_specs=pl.BlockSpec((1, tile, n_out), lambda i, j: (i, j, 0)),
        out_shape=jax.ShapeDtypeStruct((b, length, n_out), BF16),
        compiler_params=_cparams(2),
        name="pre_even" if rope is None else "pre_odd",
    )(*args)


def _interleave(main, side):
    out, done = [], 0
    for i, task in enumerate(main):
        out.append(task)
        want = (i + 1) * len(side) // len(main)
        out.extend(side[done:want])
        done = want
    return out


def _pipelined(produce, consume, items, lookahead):
    pending = {}

    def step(idx):
        if idx < len(items):
            pending[idx] = produce(*items[idx])
        if idx >= lookahead:
            done = idx - lookahead
            consume(*items[done], pending.pop(done))

    return [functools.partial(step, idx) for idx in range(len(items) + lookahead)]


def _even_tasks(q_ref, k_ref, v_ref, bias_ref, up_ref, uc_ref, un_ref, pw_ref, ps_ref, out_ref, ext_ref,
                *, j, n_tiles, rows, length):
    row0 = j * ROW_TILE
    win0 = jnp.clip(row0 - NA_WIN_ROWS // 2, 0, rows - NA_KEY_ROWS)
    lane = lax.broadcasted_iota(jnp.int32, (GRID_W, LANES), 1)
    low = lane < HEAD_DIM
    ones_keys = jnp.ones((NA_KEYS, LANES), BF16)

    koffs, d0s = [], []
    for r in range(ROW_TILE):
        row = row0 + r
        rs = jnp.clip(row - NA_WIN_ROWS // 2, 0, rows - NA_WIN_ROWS)
        koffs.append(pl.multiple_of((rs - win0) * GRID_W, GRID_W))
        d0s.append(rs - row + (NA_WIN_ROWS - 1))

    def scores(r, p):
        cols = slice(p * LANES, (p + 1) * LANES)
        qp = q_ref[0, r * GRID_W:(r + 1) * GRID_W, cols]
        zero = jnp.zeros_like(qp)
        qbd = jnp.concatenate([jnp.where(low, qp, zero), jnp.where(low, zero, qp)], axis=0)
        kp = k_ref[0, pl.ds(koffs[r], NA_KEYS), cols]
        s = lax.dot_general(qbd, kp, (((1,), (1,)), ((), ())), preferred_element_type=F32)
        return s + bias_ref[d0s[r], p]

    def attend(r, p, s):
        cols = slice(p * LANES, (p + 1) * LANES)
        m = jnp.max(s, axis=-1, keepdims=True)
        e = jnp.exp(s - m).astype(BF16)
        vp = jnp.concatenate([v_ref[0, pl.ds(koffs[r], NA_KEYS), cols], ones_keys], axis=1)
        oa = jnp.dot(e, vp, preferred_element_type=F32)
        o = oa[:, :LANES] * (1.0 / oa[:, LANES:])
        out_ref[r * GRID_W:(r + 1) * GRID_W, cols] = jnp.where(low, o[:GRID_W], o[GRID_W:]).astype(BF16)

    first = j == 0
    last = j == n_tiles - 1
    zpad = jnp.zeros((POOL_PAD - POOL_EDGE, 512), BF16)
    prev = up_ref[0]
    nxt = un_ref[0]
    ext_ref[0:POOL_PAD - POOL_EDGE, :] = zpad
    ext_ref[POOL_PAD - POOL_EDGE:POOL_PAD, :] = jnp.where(first, jnp.zeros_like(prev), prev)
    ext_ref[POOL_PAD:POOL_PAD + MIX_TILE, :] = uc_ref[0]
    ext_ref[POOL_PAD + MIX_TILE:POOL_PAD + MIX_TILE + POOL_EDGE, :] = jnp.where(last, jnp.zeros_like(nxt), nxt)
    ext_ref[POOL_PAD + MIX_TILE + POOL_EDGE:, :] = zpad
    band_shape = (POOL_BLK, POOL_BLK + 2 * POOL_PAD)
    delta = (lax.broadcasted_iota(jnp.int32, band_shape, 1)
             - lax.broadcasted_iota(jnp.int32, band_shape, 0) - POOL_PAD)
    bands = [jnp.where((delta >= -(w // 2)) & (delta < w - w // 2), 1.0, 0.0).astype(BF16)
             for w in POOL_WINDOWS]

    def pool_sum(half, gp):
        sums = []
        for g in (2 * gp, 2 * gp + 1):
            cols = slice(g * POOL_GROUP_DIM, (g + 1) * POOL_GROUP_DIM)
            xs = [ext_ref[blk * POOL_BLK:blk * POOL_BLK + POOL_BLK + 2 * POOL_PAD, cols]
                  for blk in (2 * half, 2 * half + 1)]
            sums.append(jnp.dot(bands[g], jnp.concatenate(xs, axis=1), preferred_element_type=F32))
        return sums

    def pool_project(half, gp, sums):
        mixed = []
        for g, acc2 in zip((2 * gp, 2 * gp + 1), sums):
            w = POOL_WINDOWS[g]
            cols = slice(g * POOL_GROUP_DIM, (g + 1) * POOL_GROUP_DIM)
            rows_g = []
            for k, blk in enumerate((2 * half, 2 * half + 1)):
                base = blk * POOL_BLK
                u = ext_ref[base + POOL_PAD:base + POOL_PAD + POOL_BLK, cols].astype(F32)
                t = lax.broadcasted_iota(jnp.int32, (POOL_BLK, 1), 0) + (j * MIX_TILE + base)
                lo = jnp.clip(t - w // 2, 0, length)
                hi = jnp.clip(t - w // 2 + w, 0, length)
                acc = acc2[:, k * POOL_GROUP_DIM:(k + 1) * POOL_GROUP_DIM]
                rows_g.append((acc / (hi - lo).astype(F32) - u).astype(BF16))
            mixed.append(jnp.concatenate(rows_g, axis=0))
        cols2 = slice(2 * gp * POOL_GROUP_DIM, 2 * (gp + 1) * POOL_GROUP_DIM)
        y = jnp.dot(jnp.concatenate(mixed, axis=1), pw_ref[gp], preferred_element_type=F32) * ps_ref[:, cols2]
        base = 2 * half * POOL_BLK
        out_ref[base:base + 2 * POOL_BLK, NA_WIDTH + cols2.start:NA_WIDTH + cols2.stop] = y.astype(BF16)

    units = [(r, p) for r in range(ROW_TILE) for p in range(NA_PAIRS)]
    pools = [(half, gp) for half in range(MIX_TILE // (2 * POOL_BLK)) for gp in range(len(POOL_WINDOWS) // 2)]
    return _interleave(_pipelined(scores, attend, units, NA_LOOKAHEAD),
                       _pipelined(pool_sum, pool_project, pools, POOL_LOOKAHEAD))


SWA_KEYS = 3 * SWA_BLOCK
SWA_WIN = MIX_TILE + 2 * SWA_BLOCK
SWA_GROUP = 4
Q_BLOCKS = MIX_TILE // SWA_BLOCK


def _odd_tasks(q_ref, kv_ref, sink_ref, mask_ref, bg_ref, up_ref, uc_ref, un_ref, cw_ref, cb_ref,
               out_ref, ext_ref, *, j, n_tiles, length):
    t0 = j * MIX_TILE
    win0 = jnp.clip(t0 - SWA_BLOCK, 0, length - SWA_WIN)
    lane = lax.broadcasted_iota(jnp.int32, (SWA_BLOCK, LANES), 1)
    qk_a = (lane // (HEAD_DIM // 2)) % 2 == 0
    v_rows = lax.broadcasted_iota(jnp.int32, (LANES, SWA_BLOCK), 0) < HEAD_DIM
    koffs, rels = [], []
    for qb in range(Q_BLOCKS):
        qs = t0 + qb * SWA_BLOCK
        ks = jnp.clip(qs - SWA_BLOCK, 0, length - SWA_KEYS)
        koffs.append(pl.multiple_of(ks - win0, SWA_BLOCK))
        rels.append((qs - ks) // SWA_BLOCK)

    def scores(qb, c):
        rows_q = slice(qb * SWA_BLOCK, (qb + 1) * SWA_BLOCK)
        qc = q_ref[0, rows_q, c * LANES:(c + 1) * LANES]
        zero = jnp.zeros_like(qc)
        qst = jnp.concatenate([jnp.where(qk_a, qc, zero), jnp.where(qk_a, zero, qc)], axis=0)
        kp = kv_ref[0, pl.ds(koffs[qb], SWA_KEYS), 0:LANES]
        return lax.dot_general(kp, qst, (((1,), (1,)), ((), ())), preferred_element_type=F32)

    def attend(qb, c, s):
        rows_q = slice(qb * SWA_BLOCK, (qb + 1) * SWA_BLOCK)
        mask = mask_ref[rels[qb]]
        s = jnp.concatenate([s[:, :SWA_BLOCK] + mask, s[:, SWA_BLOCK:] + mask], axis=1)
        sink = sink_ref[c]
        m = jnp.maximum(jnp.max(s, axis=0, keepdims=True), sink)
        e = jnp.exp(s - m)
        den = jnp.sum(e, axis=0, keepdims=True) + jnp.exp(sink - m)
        vp = kv_ref[0, pl.ds(koffs[qb], SWA_KEYS), LANES:2 * LANES]
        ot = lax.dot_general(vp, e.astype(BF16), (((0,), (0,)), ((), ())), preferred_element_type=F32)
        ot = ot * (1.0 / den)
        o = jnp.where(v_rows, ot[:, :SWA_BLOCK], ot[:, SWA_BLOCK:])
        out_ref[rows_q, c * LANES:(c + 1) * LANES] = o.T.astype(BF16)

    first = j == 0
    last = j == n_tiles - 1
    ext_ref[0:CONV_HALO, :] = jnp.where(first, 0.0, up_ref[0].astype(F32))
    ext_ref[CONV_HALO:CONV_HALO + MIX_TILE, :] = uc_ref[0].astype(F32)
    ext_ref[CONV_HALO + MIX_TILE:, :] = jnp.where(last, 0.0, un_ref[0].astype(F32))

    def conv_chunk(blk):
        rows_c = slice(blk * SWA_BLOCK, (blk + 1) * SWA_BLOCK)
        conv = cb_ref[...]
        for jj in range(3):
            off = CONV_HALO - 1 + jj + blk * SWA_BLOCK
            conv = conv + ext_ref[off:off + SWA_BLOCK, :] * cw_ref[jj:jj + 1, :]
        out_ref[rows_c, SWA_WIDTH:SWA_WIDTH + CONV_WIDTH] = (
            bg_ref[0, rows_c, :].astype(F32) * conv).astype(BF16)

    units = [(qb, c) for qb in range(Q_BLOCKS) for c in range(SWA_GROUP)]
    return _interleave(_pipelined(scores, attend, units, SWA_LOOKAHEAD),
                       [functools.partial(conv_chunk, blk) for blk in range(Q_BLOCKS)])


def _swa_masks():
    q = np.arange(SWA_BLOCK)[None, :]
    k = np.arange(SWA_KEYS)[:, None]
    slabs = [np.where(np.abs(q + rel * SWA_BLOCK - k) <= SWA_WINDOW, 0.0, -np.inf) for rel in range(3)]
    return jnp.asarray(np.stack(slabs), F32)


POST_CHUNK = 256
FF_CHUNK = 256


def _post_tasks(x_ref, mix_ref, mod_ref, g_ref, wo_ref, w1_ref, w3_ref, w2_ref, fg_ref, o_ref, hid_ref,
                *, final):
    st = {"x1": [], "x2": []}
    n_d = D_MODEL // POST_CHUNK
    n_ff = D_FF // FF_CHUNK

    def proj_out(c):
        cols = slice(c * POST_CHUNK, (c + 1) * POST_CHUNK)
        mix = jnp.dot(mix_ref[...], wo_ref[:, cols], preferred_element_type=F32)
        st["x1"].append(x_ref[0, :, cols] + mod_ref[0, 2:3, cols] * mix)

    def norm():
        x1 = jnp.concatenate(st["x1"], axis=1)
        st["x1_full"] = x1
        st["h"] = _norm_modulate(x1, g_ref[...], mod_ref[0, 3:4, :], mod_ref[0, 4:5, :]).astype(BF16)

    def up(c):
        cols = slice(c * FF_CHUNK, (c + 1) * FF_CHUNK)
        a = jnp.dot(st["h"], w1_ref[:, cols], preferred_element_type=F32)
        b = jnp.dot(st["h"], w3_ref[:, cols], preferred_element_type=F32)
        hid_ref[:, cols] = (a * jax.nn.sigmoid(a) * b).astype(BF16)

    def down(c):
        cols = slice(c * POST_CHUNK, (c + 1) * POST_CHUNK)
        f = jnp.dot(hid_ref[...], w2_ref[:, cols], preferred_element_type=F32)
        x2 = st["x1_full"][:, cols] + mod_ref[0, 5:6, cols] * f
        if final:
            st["x2"].append(x2)
        else:
            o_ref[0, :, cols] = x2

    def final_norm():
        x2 = jnp.concatenate(st["x2"], axis=1)
        o_ref[0] = x2 * lax.rsqrt(jnp.mean(x2 * x2, axis=-1, keepdims=True) + EPS) * fg_ref[...]

    tasks = [functools.partial(proj_out, c) for c in range(n_d)] + [norm]
    tasks += [functools.partial(up, c) for c in range(n_ff)]
    tasks += [functools.partial(down, c) for c in range(n_d)]
    if final:
        tasks.append(final_norm)
    return tasks


def _mixpost_kernel(*refs, n_mix, mixer, final, n_tiles, mixer_kw):
    mix_refs = refs[:n_mix]
    (x_ref, mod_ref, g_ref, wo_ref, w1_ref, w3_ref, w2_ref, fg_ref, o_ref,
     mix_new, mix_old, hid_ref, ext_ref) = refs[n_mix:]
    s = pl.program_id(0)
    last = pl.num_programs(0) - 1

    def mixer_tasks():
        return mixer(*mix_refs, mix_new, ext_ref, j=s % n_tiles, n_tiles=n_tiles, **mixer_kw)

    def post_tasks():
        return _post_tasks(x_ref, mix_old, mod_ref, g_ref, wo_ref, w1_ref, w3_ref, w2_ref, fg_ref, o_ref,
                           hid_ref, final=final)

    @pl.when(s == 0)
    def _():
        for task in mixer_tasks():
            task()

    @pl.when(jnp.logical_and(s > 0, s < last))
    def _():
        mix_old[...] = mix_new[...]
        for task in _interleave(post_tasks(), mixer_tasks()):
            task()

    @pl.when(s == last)
    def _():
        mix_old[...] = mix_new[...]
        for task in post_tasks():
            task()


def _mixpost(kind, z, mixer_consts, x, mod, g, wo, ffn, layer, final_g, final):
    w1, w3, w2 = ffn
    b, length, d = x.shape
    assert length % MIX_TILE == 0
    n_tiles = length // MIX_TILE
    total = b * n_tiles

    def mix_tile(s):
        t = jnp.minimum(s, total - 1)
        return t // n_tiles, t % n_tiles

    def post_tile(s):
        t = jnp.maximum(s - 1, 0)
        return t // n_tiles, t % n_tiles

    def tile_spec(width, col_block):
        return pl.BlockSpec((1, MIX_TILE, width), lambda s: (*mix_tile(s), col_block))

    def halo_specs(rows, col_block):
        per_tile = MIX_TILE // rows
        n_halo = length // rows

        def before(s):
            i, j = mix_tile(s)
            return i, jnp.maximum(j * per_tile - 1, 0), col_block

        def after(s):
            i, j = mix_tile(s)
            return i, jnp.minimum((j + 1) * per_tile, n_halo - 1), col_block

        return (pl.BlockSpec((1, rows, 512), before), pl.BlockSpec((1, rows, 512), after))

    if kind == "even":
        rows = length // GRID_W
        assert rows >= NA_KEY_ROWS
        bias, pool_w, pool_scale = mixer_consts

        def win_map(s, col0):
            i, j = mix_tile(s)
            win0 = jnp.clip(j * ROW_TILE - NA_WIN_ROWS // 2, 0, rows - NA_KEY_ROWS)
            return i, pl.multiple_of(win0 * GRID_W, GRID_W), col0

        win = (pl.Element(1), pl.Element(NA_KEY_ROWS * GRID_W), pl.Element(NA_WIDTH))
        u_before, u_after = halo_specs(POOL_EDGE, 3)
        mix_specs = [
            tile_spec(NA_WIDTH, 0),
            pl.BlockSpec(win, lambda s: win_map(s, NA_WIDTH)),
            pl.BlockSpec(win, lambda s: win_map(s, 2 * NA_WIDTH)),
            _const_spec(bias.shape),
            u_before, tile_spec(512, 3), u_after,
            _const_spec(pool_w.shape),
            _const_spec((1, 512)),
        ]
        mix_args = [z, z, z, bias, z, z, z, pool_w, pool_scale.reshape(1, 512)]
        mixer, mixer_kw = _even_tasks, dict(rows=rows, length=length)
        ext = pltpu.VMEM((MIX_TILE + 2 * POOL_PAD, 512), BF16)
    else:
        assert length >= SWA_WIN
        sink_rows, conv_w, conv_b = mixer_consts

        def win_map(s):
            i, j = mix_tile(s)
            start = jnp.clip(j * MIX_TILE - SWA_BLOCK, 0, length - SWA_WIN)
            return i, pl.multiple_of(start, SWA_BLOCK), ODD_OUT - 2 * SWA_KV_WIDTH

        u_before, u_after = halo_specs(CONV_HALO, 2)
        mix_specs = [
            tile_spec(512, 0),
            pl.BlockSpec((pl.Element(1), pl.Element(SWA_WIN), pl.Element(2 * SWA_KV_WIDTH)), win_map),
            _const_spec((SWA_GROUP, 1, 2 * SWA_BLOCK)),
            _const_spec((3, SWA_KEYS, SWA_BLOCK)),
            tile_spec(512, 1),
            u_before, tile_spec(512, 2), u_after,
            _const_spec((3, CONV_WIDTH)),
            _const_spec((1, CONV_WIDTH)),
        ]
        mix_args = [z, z, sink_rows, _swa_masks(), z, z, z, z, conv_w, conv_b.reshape(1, CONV_WIDTH)]
        mixer, mixer_kw = _odd_tasks, dict(length=length)
        ext = pltpu.VMEM((MIX_TILE + 2 * CONV_HALO, 512), F32)

    post_specs = [
        pl.BlockSpec((1, MIX_TILE, d), lambda s: (*post_tile(s), 0)),
        pl.BlockSpec((1, 6, d), lambda s: (post_tile(s)[0], 0, 0)),
        _const_spec((1, d)),
        _const_spec(wo.shape),
        _layer_spec(w1.shape, layer),
        _layer_spec(w3.shape, layer),
        _layer_spec(w2.shape, layer),
        _const_spec((1, d)),
    ]
    post_args = [x, mod, g.reshape(1, d), wo, w1, w3, w2, final_g.reshape(1, d)]
    return pl.pallas_call(
        functools.partial(_mixpost_kernel, n_mix=len(mix_specs), mixer=mixer, final=final, n_tiles=n_tiles,
                          mixer_kw=mixer_kw),
        grid=(total + 1,),
        in_specs=mix_specs + post_specs,
        out_specs=pl.BlockSpec((1, MIX_TILE, d), lambda s: (*post_tile(s), 0)),
        out_shape=jax.ShapeDtypeStruct((b, length, d), F32),
        scratch_shapes=[pltpu.VMEM((MIX_TILE, D_MODEL), BF16), pltpu.VMEM((MIX_TILE, D_MODEL), BF16),
                        pltpu.VMEM((MIX_TILE, D_FF), BF16), ext],
        compiler_params=_cparams(1),
        name="mixpost_" + kind,
    )(*mix_args, *post_args)


def _permute_odd_in(w):
    d = w.shape[0]
    half = HEAD_DIM // 2
    q = w[:, :SWA_WIDTH].reshape(d, 2, SWA_GROUP, 2, half)
    q = jnp.transpose(q, (0, 2, 3, 1, 4)).reshape(d, SWA_WIDTH)
    k = w[:, SWA_WIDTH:SWA_WIDTH + SWA_KV_WIDTH].reshape(d, 2, 2, half)
    k = jnp.transpose(k, (0, 2, 1, 3)).reshape(d, SWA_KV_WIDTH)
    return jnp.concatenate([q, k, w[:, SWA_WIDTH + SWA_KV_WIDTH:]], axis=1)


def _permute_odd_out(w):
    d = w.shape[1]
    a = w[:SWA_WIDTH].reshape(2, SWA_GROUP, HEAD_DIM, d)
    a = jnp.transpose(a, (1, 0, 2, 3)).reshape(SWA_WIDTH, d)
    return jnp.concatenate([a, w[SWA_WIDTH:]], axis=0)


def _fold_qk_scale(w_in, q_width):
    scale = HEAD_DIM ** -0.5
    assert np.log2(scale) == round(np.log2(scale))
    col = jnp.arange(w_in.shape[-1]) < q_width
    return w_in * jnp.where(col, scale, 1.0).astype(w_in.dtype)


def _pool_pairs(pw):
    zero = jnp.zeros_like(pw[0])
    return jnp.stack([jnp.block([[pw[2 * i], zero], [zero, pw[2 * i + 1]]]) for i in range(pw.shape[0] // 2)])


def _sink_rows(sink):
    per_head = jnp.repeat(sink.astype(F32).reshape(2, SWA_GROUP, 1), SWA_BLOCK, axis=2)
    return jnp.transpose(per_head, (1, 0, 2)).reshape(SWA_GROUP, 1, 2 * SWA_BLOCK)


def _trunk(x, mods, p):
    depth = p["ffn"][0].shape[0]
    for i in range(depth):
        jdx = i // 2
        mod = mods[i]
        if i % 2 == 0:
            kind = "even"
            z = _pre(x, mod, p["norm_g"][i, 0], p["even_w_in"][jdx])
            consts = (p["na_bias"][jdx], p["pool_w"][jdx], p["pool_scale"][jdx])
            wo = p["even_w_out"][jdx]
        else:
            kind = "odd"
            z = _pre(x, mod, p["norm_g"][i, 0], p["odd_w_in"][jdx], rope=p["rope"])
            consts = (p["sink_rows"][jdx], p["conv_w"][jdx], p["conv_b"][jdx])
            wo = p["odd_w_out"][jdx]
        x = _mixpost(kind, z, consts, x, mod, p["norm_g"][i, 1], wo, p["ffn"], i, p["final_g"],
                     final=(i == depth - 1))
    return x


def kernel(x_prompt, x_sample, c_prompt, c_sample, ada_w, ada_b, norm_g, final_g, ffn_w1, ffn_w3, ffn_w2,
           even_w_in, na_rpb, pool_w, pool_scale, even_w_out, odd_w_in, swa_sink, conv_w, conv_b, odd_w_out):
    depth = ada_w.shape[0]
    nb_p, nb_s = c_prompt.shape[0], c_sample.shape[0]
    pad = (-(nb_p + nb_s)) % 8
    c_all = jnp.concatenate([c_prompt, c_sample, jnp.zeros((pad, D_MODEL), F32)], axis=0)
    mod = _ada_mod(c_all, ada_w, ada_b)
    mods_p = [mod[i, :nb_p].reshape(nb_p, 6, D_MODEL) for i in range(depth)]
    mods_s = [mod[i, nb_p:nb_p + nb_s].reshape(nb_s, 6, D_MODEL) for i in range(depth)]

    max_len = max(x_prompt.shape[1], x_sample.shape[1])
    n_even, n_odd = even_w_in.shape[0], odd_w_in.shape[0]
    p = {
        "norm_g": norm_g, "final_g": final_g,
        "ffn": (ffn_w1.astype(BF16), ffn_w3.astype(BF16), ffn_w2.astype(BF16)),
        "even_w_in": [_fold_qk_scale(even_w_in[j], NA_WIDTH).astype(BF16) for j in range(n_even)],
        "even_w_out": [even_w_out[j].astype(BF16) for j in range(n_even)],
        "na_bias": [_bias_slabs(na_rpb[j]) for j in range(n_even)],
        "pool_w": [_pool_pairs(pool_w[j]).astype(BF16) for j in range(n_even)],
        "pool_scale": pool_scale,
        "odd_w_in": [_permute_odd_in(_fold_qk_scale(odd_w_in[j], SWA_WIDTH)).astype(BF16) for j in range(n_odd)],
        "odd_w_out": [_permute_odd_out(odd_w_out[j]).astype(BF16) for j in range(n_odd)],
        "sink_rows": [_sink_rows(swa_sink[j]) for j in range(n_odd)],
        "conv_w": conv_w, "conv_b": conv_b,
        "rope": _rope_tables(max_len),
    }
    return (_trunk(x_prompt, mods_p, p), _trunk(x_sample, mods_s, p))
```
